```python
import jax, jax.numpy as jnp
from jax import lax
import numpy as np


D_MODEL = 1024
BATCH = 1
SEQ = 16384
DEPTH = 1
DEC_BATCH = 8
DEC_SEQ = 4096
PAST_LEN = 128

HEAD_DIM = 64
N_FOURIER_GROUPS = 8
FOURIER_WIDTH = N_FOURIER_GROUPS * HEAD_DIM
N_Q_HEADS = 8
N_KV_HEADS = 2
Q_PER_KV = N_Q_HEADS // N_KV_HEADS
ATTN_WIDTH = N_Q_HEADS * HEAD_DIM
KV_WIDTH = N_KV_HEADS * HEAD_DIM
MIX_WIDTH = FOURIER_WIDTH + ATTN_WIDTH
IN_PROJ_WIDTH = FOURIER_WIDTH + ATTN_WIDTH + 2 * KV_WIDTH
WINDOW = 128
BLOCK = 128
D_FF = 2816
ROPE_THETA = 10000.0
RMS_EPS = 1e-6
NEG_INF = -1e30

kernel_name = 'hybrid_fourier_window_gqa_macaron_encoder'


def rmsnorm(x, g):
    xf = x.astype(jnp.float32)
    y = xf * lax.rsqrt(jnp.mean(xf * xf, axis=-1, keepdims=True) + RMS_EPS)
    return (y * g.astype(jnp.float32)).astype(x.dtype)


def swiglu(x, w_gate, w_up, w_down):
    return (jax.nn.silu(x @ w_gate) * (x @ w_up)) @ w_down


def rope(x, positions):
    half = HEAD_DIM // 2
    inv_freq = ROPE_THETA ** (-jnp.arange(half, dtype=jnp.float32) / half)
    ang = positions.astype(jnp.float32)[:, None] * inv_freq[None, :]
    cos = jnp.cos(ang)[None, :, None, :]
    sin = jnp.sin(ang)[None, :, None, :]
    xf = x.astype(jnp.float32)
    x1, x2 = xf[..., :half], xf[..., half:]
    out = jnp.concatenate([x1 * cos - x2 * sin, x2 * cos + x1 * sin], axis=-1)
    return out.astype(x.dtype)


def fourier_mixer(u, w_fourier):
    B, S = u.shape[0], u.shape[1]
    ug = u.reshape(B, S, N_FOURIER_GROUPS, HEAD_DIM).astype(jnp.float32)
    f = jnp.fft.fft2(ug, axes=(1, 3), norm='ortho').real.astype(u.dtype)
    y = jnp.einsum('bsgc,gcd->bsgd', f, w_fourier)
    return y.reshape(B, S, FOURIER_WIDTH)


def band_blocks(t, n_blocks):
    B = t.shape[0]
    tp = jnp.pad(t, ((0, 0), (BLOCK, BLOCK), (0, 0), (0, 0)))
    tp = tp.reshape(B, n_blocks + 2, BLOCK, N_KV_HEADS, HEAD_DIM)
    return jnp.concatenate([tp[:, :-2], tp[:, 1:-1], tp[:, 2:]], axis=2)


def window_attention(q, k, v, sink):
    B, S = q.shape[0], q.shape[1]
    nb = S // BLOCK
    qb = q.reshape(B, nb, BLOCK, N_KV_HEADS, Q_PER_KV, HEAD_DIM).astype(jnp.float32)
    kb = band_blocks(k, nb).astype(jnp.float32)
    vb = band_blocks(v, nb).astype(jnp.float32)
    scores = jnp.einsum('bnqkgd,bnskd->bnkgqs', qb, kb) * (HEAD_DIM ** -0.5)
    qi = jnp.arange(BLOCK)[:, None]
    kj = jnp.arange(3 * BLOCK)[None, :]
    band_ok = jnp.abs(kj - BLOCK - qi) <= WINDOW
    kpos = jnp.arange(nb)[:, None, None] * BLOCK - BLOCK + kj[None]
    mask = band_ok[None] & (kpos >= 0) & (kpos < S)
    scores = jnp.where(mask[None, :, None, None], scores, NEG_INF)
    sink_l = sink.astype(jnp.float32).reshape(N_KV_HEADS, Q_PER_KV)[None, None, :, :, None, None]
    m = jnp.maximum(jnp.max(scores, axis=-1, keepdims=True), sink_l)
    p = jnp.exp(scores - m)
    p = p / (jnp.sum(p, axis=-1, keepdims=True) + jnp.exp(sink_l - m))
    out = jnp.einsum('bnkgqs,bnskd->bnqkgd', p, vb)
    return out.reshape(B, S, ATTN_WIDTH).astype(q.dtype)


def encoder_layer(x, g_ffn1, w1_gate, w1_up, w1_down, g_mix, w_in, w_fourier, attn_sink, w_out,
                  g_ffn2, w2_gate, w2_up, w2_down):
    B, S = x.shape[0], x.shape[1]
    x = x + 0.5 * swiglu(rmsnorm(x, g_ffn1), w1_gate, w1_up, w1_down)
    h = rmsnorm(x, g_mix)
    proj = h @ w_in
    u = proj[..., :FOURIER_WIDTH]
    q = proj[..., FOURIER_WIDTH:FOURIER_WIDTH + ATTN_WIDTH].reshape(B, S, N_Q_HEADS, HEAD_DIM)
    k = proj[..., FOURIER_WIDTH + ATTN_WIDTH:FOURIER_WIDTH + ATTN_WIDTH + KV_WIDTH].reshape(B, S, N_KV_HEADS, HEAD_DIM)
    v = proj[..., FOURIER_WIDTH + ATTN_WIDTH + KV_WIDTH:].reshape(B, S, N_KV_HEADS, HEAD_DIM)
    pos = jnp.arange(S)
    q = rope(q, pos)
    k = rope(k, pos)
    y_f = fourier_mixer(u, w_fourier)
    y_a = window_attention(q, k, v, attn_sink)
    x = x + jnp.concatenate([y_f, y_a], axis=-1) @ w_out
    x = x + 0.5 * swiglu(rmsnorm(x, g_ffn2), w2_gate, w2_up, w2_down)
    return x


def trunk(x, g_ffn1, w1_gate, w1_up, w1_down, g_mix, w_in, w_fourier, attn_sink, w_out,
          g_ffn2, w2_gate, w2_up, w2_down, g_final):
    for l in range(DEPTH):
        x = encoder_layer(x, g_ffn1[l], w1_gate[l], w1_up[l], w1_down[l], g_mix[l], w_in[l],
                          w_fourier[l], attn_sink[l], w_out[l], g_ffn2[l], w2_gate[l], w2_up[l], w2_down[l])
    return rmsnorm(x, g_final)


def setup_inputs(seed: int = 0) -> dict:
    key = jax.random.key(seed)
    ks = jax.random.split(key, 20)
    f32 = jnp.float32

    def w(k, shape, fan_in):
        return jax.random.normal(k, shape, f32) * (fan_in ** -0.5)

    def gain(k, shape):
        return 1.0 + 0.02 * jax.random.normal(k, shape, f32)

    return {
        'x_prompt': jax.random.normal(ks[0], (BATCH, SEQ, D_MODEL), f32),
        'x_sample': jax.random.normal(ks[1], (DEC_BATCH, DEC_SEQ, D_MODEL), f32),
        'g_ffn1': gain(ks[2], (DEPTH, D_MODEL)),
        'w1_gate': w(ks[3], (DEPTH, D_MODEL, D_FF), D_MODEL),
        'w1_up': w(ks[4], (DEPTH, D_MODEL, D_FF), D_MODEL),
        'w1_down': w(ks[5], (DEPTH, D_FF, D_MODEL), D_FF),
        'g_mix': gain(ks[6], (DEPTH, D_MODEL)),
        'w_in': w(ks[7], (DEPTH, D_MODEL, IN_PROJ_WIDTH), D_MODEL),
        'w_fourier': w(ks[8], (DEPTH, N_FOURIER_GROUPS, HEAD_DIM, HEAD_DIM), HEAD_DIM),
        'attn_sink': 0.5 * jax.random.normal(ks[9], (DEPTH, N_Q_HEADS), f32),
        'w_out': w(ks[10], (DEPTH, MIX_WIDTH, D_MODEL), MIX_WIDTH),
        'g_ffn2': gain(ks[11], (DEPTH, D_MODEL)),
        'w2_gate': w(ks[12], (DEPTH, D_MODEL, D_FF), D_MODEL),
        'w2_up': w(ks[13], (DEPTH, D_MODEL, D_FF), D_MODEL),
        'w2_down': w(ks[14], (DEPTH, D_FF, D_MODEL), D_FF),
        'g_final': gain(ks[15], (D_MODEL,)),
    }


def reference(x_prompt, x_sample, g_ffn1, w1_gate, w1_up, w1_down, g_mix, w_in, w_fourier, attn_sink,
              w_out, g_ffn2, w2_gate, w2_up, w2_down, g_final):
    y_prompt = trunk(x_prompt, g_ffn1, w1_gate, w1_up, w1_down, g_mix, w_in, w_fourier, attn_sink,
                     w_out, g_ffn2, w2_gate, w2_up, w2_down, g_final)
    y_sample = trunk(x_sample, g_ffn1, w1_gate, w1_up, w1_down, g_mix, w_in, w_fourier, attn_sink,
                     w_out, g_ffn2, w2_gate, w2_up, w2_down, g_final)
    return (y_prompt, y_sample)
```

```python
import functools

import numpy as np
import jax
import jax.numpy as jnp
from jax import lax
from jax.experimental import pallas as pl
from jax.experimental.pallas import tpu as pltpu

D_MODEL = 1024
HEAD_DIM = 64
N_FOURIER_GROUPS = 8
FOURIER_WIDTH = N_FOURIER_GROUPS * HEAD_DIM
N_Q_HEADS = 8
N_KV_HEADS = 2
Q_PER_KV = N_Q_HEADS // N_KV_HEADS
ATTN_WIDTH = N_Q_HEADS * HEAD_DIM
KV_WIDTH = N_KV_HEADS * HEAD_DIM
D_FF = 2816
WINDOW = 128
BLOCK = 128
ROPE_THETA = 10000.0
RMS_EPS = 1e-6
NEG_INF = -1e30

SUBLANES = 8
LANES = 128
PROJ_WIDTH = 2 * FOURIER_WIDTH + ATTN_WIDTH + 2 * KV_WIDTH
VMEM_LIMIT_BYTES = 56 * 1024 * 1024

F32 = jnp.float32
BF16 = jnp.bfloat16


def _rms(x, g):
    return x * lax.rsqrt(jnp.mean(x * x, axis=-1, keepdims=True) + RMS_EPS) * g


def _const_spec(shape):
    zeros = (0,) * len(shape)
    return pl.BlockSpec(shape, lambda *_: zeros, pipeline_mode=pl.Buffered(1))


def _params(n_axes):
    return pltpu.CompilerParams(dimension_semantics=("arbitrary",) * n_axes,
                                vmem_limit_bytes=VMEM_LIMIT_BYTES)


def _fold_kernel(wu_ref, wf_ref, cc_ref, sc_ref, o_ref):
    hi = lax.Precision.HIGHEST
    for g in range(N_FOURIER_GROUPS):
        wf = wf_ref[g]
        pr = jnp.dot(cc_ref[...], wf, precision=hi, preferred_element_type=F32)
        pi = jnp.dot(sc_ref[...], wf, precision=hi, preferred_element_type=F32)
        wug = wu_ref[:, g * HEAD_DIM:(g + 1) * HEAD_DIM]
        o_ref[:, g * HEAD_DIM:(g + 1) * HEAD_DIM] = jnp.dot(wug, pr, precision=hi, preferred_element_type=F32)
        o_ref[:, FOURIER_WIDTH + g * HEAD_DIM:FOURIER_WIDTH + (g + 1) * HEAD_DIM] = -jnp.dot(
            wug, pi, precision=hi, preferred_element_type=F32)


def _fold_weights(w_in_u, w_fourier):
    c = np.arange(HEAD_DIM)
    ang = 2.0 * np.pi * np.outer(c, c) / HEAD_DIM
    scale = HEAD_DIM ** -0.5
    cc = jnp.asarray(np.cos(ang) * scale, F32)
    sc = jnp.asarray(np.sin(ang) * scale, F32)
    return pl.pallas_call(
        _fold_kernel,
        out_shape=jax.ShapeDtypeStruct((D_MODEL, 2 * FOURIER_WIDTH), F32),
        name="fold",
    )(w_in_u, w_fourier, cc, sc)


def _ffn_kernel(x_ref, g_ref, wg_ref, wu_ref, wd_ref, gf_ref, o_ref, *, final_norm):
    x = x_ref[...]
    xn = _rms(x, g_ref[...]).astype(BF16)
    gate = jnp.dot(xn, wg_ref[...], preferred_element_type=F32)
    up = jnp.dot(xn, wu_ref[...], preferred_element_type=F32)
    act = (gate * jax.nn.sigmoid(gate) * up).astype(BF16)
    y = x + 0.5 * jnp.dot(act, wd_ref[...], preferred_element_type=F32)
    if final_norm:
        y = _rms(y, gf_ref[...])
    o_ref[...] = y


def _ffn(x, g, wg, wu, wd, gf, *, final_norm, tm=512):
    t = x.shape[0]
    return pl.pallas_call(
        functools.partial(_ffn_kernel, final_norm=final_norm),
        grid=(t // tm,),
        in_specs=[
            pl.BlockSpec((tm, D_MODEL), lambda i: (i, 0)),
            _const_spec((1, D_MODEL)),
            _const_spec((D_MODEL, D_FF)),
            _const_spec((D_MODEL, D_FF)),
            _const_spec((D_FF, D_MODEL)),
            _const_spec((1, D_MODEL)),
        ],
        out_specs=pl.BlockSpec((tm, D_MODEL), lambda i: (i, 0)),
        out_shape=jax.ShapeDtypeStruct((t, D_MODEL), F32),
        compiler_params=_params(1),
        name="ffn_final" if final_norm else "ffn",
    )(x, g, wg, wu, wd, gf)


def _inproj_kernel(x_ref, g_ref, w_ref, cos_ref, sin_ref, twc_ref, tws_ref, d1_ref,
                   t_ref, q_ref, k_ref, v_ref, z_scr, *, n1):
    tm = n1 * SUBLANES
    x = x_ref[...].reshape(tm, D_MODEL)
    hb = _rms(x, g_ref[...]).astype(BF16)
    proj = jnp.dot(hb, w_ref[...], preferred_element_type=F32)
    for c in range(2 * FOURIER_WIDTH // LANES):
        z_scr[c] = proj[:, c * LANES:(c + 1) * LANES]

    cos = cos_ref[...].reshape(tm, LANES)
    sin = sin_ref[...].reshape(tm, LANES)
    lane = lax.broadcasted_iota(jnp.int32, (tm, LANES), 1)
    first_half = (lane % HEAD_DIM) < (HEAD_DIM // 2)

    def rope(xc):
        rot = jnp.where(first_half, pltpu.roll(xc, LANES - HEAD_DIM // 2, 1), pltpu.roll(xc, HEAD_DIM // 2, 1))
        return xc * cos + rot * sin

    q0 = 2 * FOURIER_WIDTH
    for c in range(ATTN_WIDTH // LANES):
        qc = rope(proj[:, q0 + c * LANES:q0 + (c + 1) * LANES]) * (HEAD_DIM ** -0.5)
        q_ref[:, :, c * LANES:(c + 1) * LANES] = qc.reshape(n1, SUBLANES, LANES)
    k0 = q0 + ATTN_WIDTH
    k_ref[...] = rope(proj[:, k0:k0 + KV_WIDTH]).reshape(n1, SUBLANES, KV_WIDTH)
    v_ref[...] = proj[:, k0 + KV_WIDTH:k0 + 2 * KV_WIDTH].reshape(n1, SUBLANES, KV_WIDTH)

    d1 = d1_ref[...]
    for j in range(SUBLANES):
        rows = pl.ds(j, n1, stride=SUBLANES)
        nch = FOURIER_WIDTH // LANES
        zr = jnp.concatenate([z_scr[c, rows, :] for c in range(nch)], axis=1)
        zi = jnp.concatenate([z_scr[nch + c, rows, :] for c in range(nch)], axis=1)
        st = jnp.concatenate([zr, zi], axis=0).astype(BF16)
        t = jnp.dot(d1, st, preferred_element_type=F32)
        tr, ti = t[:n1], t[n1:]
        c4 = jnp.concatenate([twc_ref[j]] * (FOURIER_WIDTH // LANES), axis=1)
        s4 = jnp.concatenate([tws_ref[j]] * (FOURIER_WIDTH // LANES), axis=1)
        t_ref[j, :n1, :] = tr * c4 + ti * s4
        t_ref[j, n1:, :] = ti * c4 - tr * s4


def _inproj(x1, g, w, cos_t, sin_t, twc, tws, d1, *, n1, n2):
    b = x1.shape[0]
    blk = lambda w_: pl.BlockSpec((None, n1, SUBLANES, w_), lambda bi, i: (bi, 0, i, 0))
    tab = pl.BlockSpec((n1, SUBLANES, LANES), lambda bi, i: (0, i, 0))
    tw = pl.BlockSpec((SUBLANES, n1, LANES), lambda bi, i: (i, 0, 0))
    return pl.pallas_call(
        functools.partial(_inproj_kernel, n1=n1),
        grid=(b, n2 // SUBLANES),
        in_specs=[blk(D_MODEL), _const_spec((1, D_MODEL)), _const_spec((D_MODEL, PROJ_WIDTH)),
                  tab, tab, tw, tw, _const_spec((2 * n1, 2 * n1))],
        out_specs=[
            pl.BlockSpec((None, SUBLANES, 2 * n1, FOURIER_WIDTH), lambda bi, i: (bi, i, 0, 0)),
            blk(ATTN_WIDTH), blk(KV_WIDTH), blk(KV_WIDTH),
        ],
        out_shape=[
            jax.ShapeDtypeStruct((b, n2, 2 * n1, FOURIER_WIDTH), F32),
            jax.ShapeDtypeStruct((b, n1, n2, ATTN_WIDTH), F32),
            jax.ShapeDtypeStruct((b, n1, n2, KV_WIDTH), F32),
            jax.ShapeDtypeStruct((b, n1, n2, KV_WIDTH), F32),
        ],
        scratch_shapes=[pltpu.VMEM((2 * FOURIER_WIDTH // LANES, n1 * SUBLANES, LANES), F32)],
        compiler_params=_params(2),
        name="inproj",
    )(x1, g, w, cos_t, sin_t, twc, tws, d1)


def _attn_kernel(sink_ref, q_ref, kp_ref, kc_ref, kn_ref, vp_ref, vc_ref, vn_ref, o_ref, *, nb, nblk_seq):
    i = pl.program_id(0)
    rows = (nb + 2) * BLOCK
    kfull = jnp.concatenate([kp_ref[...], kc_ref[...], kn_ref[...]], axis=0)
    vfull = jnp.concatenate([vp_ref[...], vc_ref[...], vn_ref[...]], axis=0)
    lo = lax.broadcasted_iota(jnp.int32, (rows, LANES), 1) < HEAD_DIM

    def dup(x):
        r = pltpu.roll(x, HEAD_DIM, 1)
        return jnp.where(lo, x, r), jnp.where(lo, r, x)

    kk = dup(kfull)
    vv = dup(vfull)
    wkeys = 3 * BLOCK
    grp = lax.broadcasted_iota(jnp.int32, (wkeys, Q_PER_KV * HEAD_DIM), 1) // HEAD_DIM
    sidx = lax.broadcasted_iota(jnp.int32, (BLOCK, wkeys), 1)
    qidx = lax.broadcasted_iota(jnp.int32, (BLOCK, wkeys), 0)
    band = jnp.abs(sidx - BLOCK - qidx) <= WINDOW

    def blockdiag(x):
        wide = jnp.concatenate([x, x], axis=1)
        return jnp.concatenate([jnp.where(grp == g, wide, 0.0) for g in range(Q_PER_KV)], axis=0).astype(BF16)

    for n in range(nb):
        pb = (i * nb + n) % nblk_seq
        s_lo = jnp.where(pb > 0, 0, BLOCK)
        s_hi = jnp.where(pb < nblk_seq - 1, wkeys, 2 * BLOCK)
        mask = band & (sidx >= s_lo) & (sidx < s_hi)
        for kh in range(N_KV_HEADS):
            kbd = blockdiag(kk[kh][n * BLOCK:n * BLOCK + wkeys])
            vbd = blockdiag(vv[kh][n * BLOCK:n * BLOCK + wkeys])
            c0 = kh * Q_PER_KV * HEAD_DIM
            qh = q_ref[n * BLOCK:(n + 1) * BLOCK, c0:c0 + Q_PER_KV * HEAD_DIM].astype(BF16)
            s_all = lax.dot_general(qh, kbd, (((1,), (1,)), ((), ())), preferred_element_type=F32)
            ps = []
            for g in range(Q_PER_KV):
                sg = jnp.where(mask, s_all[:, g * wkeys:(g + 1) * wkeys], NEG_INF)
                sink = sink_ref[kh * Q_PER_KV + g]
                m = jnp.maximum(jnp.max(sg, axis=-1, keepdims=True), sink)
                p = jnp.exp(sg - m)
                den = jnp.sum(p, axis=-1, keepdims=True) + jnp.exp(sink - m)
                ps.append((p / den).astype(BF16))
            p_all = jnp.concatenate(ps, axis=1)
            o_ref[n * BLOCK:(n + 1) * BLOCK, c0:c0 + Q_PER_KV * HEAD_DIM] = jnp.dot(
                p_all, vbd, preferred_element_type=F32)


def _attention(q, k, v, sink, *, seq, nb=4):
    t = q.shape[0]
    nblk = t // BLOCK
    tq = nb * BLOCK
    cur = lambda w_: pl.BlockSpec((tq, w_), lambda i: (i, 0))
    prev = pl.BlockSpec((BLOCK, KV_WIDTH), lambda i: (jnp.maximum(i * nb - 1, 0), 0))
    nxt = pl.BlockSpec((BLOCK, KV_WIDTH), lambda i: (jnp.minimum(i * nb + nb, nblk - 1), 0))
    return pl.pallas_call(
        functools.partial(_attn_kernel, nb=nb, nblk_seq=seq // BLOCK),
        grid=(t // tq,),
        in_specs=[pl.BlockSpec(memory_space=pltpu.SMEM), cur(ATTN_WIDTH),
                  prev, cur(KV_WIDTH), nxt, prev, cur(KV_WIDTH), nxt],
        out_specs=cur(ATTN_WIDTH),
        out_shape=jax.ShapeDtypeStruct((t, ATTN_WIDTH), F32),
        compiler_params=_params(1),
        name="attention",
    )(sink, q, k, k, k, v, v, v)


def _mixout_kernel(tr_ref, ti_ref, x_ref, ya_ref, d2_ref, wo_ref, o_ref, tr_scr, ti_scr, yf_scr, *, n2):
    tm = n2 * SUBLANES
    nch = FOURIER_WIDTH // LANES
    for c in range(nch):
        tr_scr[c] = tr_ref[:, :, c * LANES:(c + 1) * LANES].reshape(tm, LANES)
        ti_scr[c] = ti_ref[:, :, c * LANES:(c + 1) * LANES].reshape(tm, LANES)
    d2 = d2_ref[...]
    for j in range(SUBLANES):
        rows = pl.ds(j, n2, stride=SUBLANES)
        tr = jnp.concatenate([tr_scr[c, rows, :] for c in range(nch)], axis=1)
        ti = jnp.concatenate([ti_scr[c, rows, :] for c in range(nch)], axis=1)
        st = jnp.concatenate([tr, ti], axis=0).astype(BF16)
        yfj = jnp.dot(d2, st, preferred_element_type=F32)
        for c in range(nch):
            yf_scr[c, rows, :] = yfj[:, c * LANES:(c + 1) * LANES]
    yf = jnp.concatenate([yf_scr[c] for c in range(nch)], axis=1).astype(BF16)
    ya = ya_ref[...].reshape(tm, ATTN_WIDTH).astype(BF16)
    y = (jnp.dot(yf, wo_ref[:FOURIER_WIDTH, :], preferred_element_type=F32)
         + jnp.dot(ya, wo_ref[FOURIER_WIDTH:, :], preferred_element_type=F32))
    o_ref[...] = (x_ref[...].reshape(tm, D_MODEL) + y).reshape(n2, SUBLANES, D_MODEL)


def _mixout(t_arr, x1, ya, d2, wo, *, n1, n2):
    b = x1.shape[0]
    blk = lambda w_: pl.BlockSpec((None, n2, SUBLANES, w_), lambda bi, i: (bi, 0, i, 0))
    t_im = pl.BlockSpec((None, n2, SUBLANES, FOURIER_WIDTH), lambda bi, i: (bi, 0, n1 // SUBLANES + i, 0))
    tm = n2 * SUBLANES
    return pl.pallas_call(
        functools.partial(_mixout_kernel, n2=n2),
        grid=(b, n1 // SUBLANES),
        in_specs=[blk(FOURIER_WIDTH), t_im, blk(D_MODEL), blk(ATTN_WIDTH),
                  _const_spec((n2, 2 * n2)), _const_spec((2 * FOURIER_WIDTH, D_MODEL))],
        out_specs=blk(D_MODEL),
        out_shape=jax.ShapeDtypeStruct((b, n2, n1, D_MODEL), F32),
        scratch_shapes=[pltpu.VMEM((FOURIER_WIDTH // LANES, tm, LANES), F32)] * 3,
        compiler_params=_params(2),
        name="mixout",
    )(t_arr, t_arr, x1, ya, d2, wo)


def _dft_tables(n1, n2):
    s = n1 * n2
    a1 = 2.0 * np.pi * np.outer(np.arange(n1), np.arange(n1)) / n1
    c1, s1 = np.cos(a1) / np.sqrt(n1), np.sin(a1) / np.sqrt(n1)
    d1 = np.block([[c1, s1], [-s1, c1]])
    a2 = 2.0 * np.pi * np.outer(np.arange(n2), np.arange(n2)) / n2
    d2 = np.concatenate([np.cos(a2), np.sin(a2)], axis=1) / np.sqrt(n2)
    m = (jnp.arange(n2, dtype=jnp.int32)[:, None] * jnp.arange(n1, dtype=jnp.int32)[None, :]) % s
    ang = m.astype(F32) * (2.0 * np.pi / s)
    twc = jnp.broadcast_to(jnp.cos(ang)[:, :, None], (n2, n1, LANES))
    tws = jnp.broadcast_to(jnp.sin(ang)[:, :, None], (n2, n1, LANES))
    return jnp.asarray(d1, F32).astype(BF16), jnp.asarray(d2, F32).astype(BF16), twc, tws


def _rope_tables(seq, n1, n2):
    half = HEAD_DIM // 2
    inv_freq = ROPE_THETA ** (-jnp.arange(half, dtype=F32) / half)
    ang = jnp.arange(seq).astype(F32)[:, None] * inv_freq[None, :]
    cos, sin = jnp.cos(ang), jnp.sin(ang)
    reps = LANES // HEAD_DIM
    cos_t = jnp.tile(jnp.concatenate([cos, cos], axis=1), (1, reps))
    sin_t = jnp.tile(jnp.concatenate([-sin, sin], axis=1), (1, reps))
    return cos_t.reshape(n1, n2, LANES), sin_t.reshape(n1, n2, LANES)


def _trunk(x, w, *, n1, n2):
    b, seq, _ = x.shape
    t = b * seq
    d1, d2, twc, tws = _dft_tables(n1, n2)
    cos_t, sin_t = _rope_tables(seq, n1, n2)
    x1 = _ffn(x.reshape(t, D_MODEL), w["g_ffn1"], w["w1_gate"], w["w1_up"], w["w1_down"], w["g_final"],
              final_norm=False)
    t_arr, q, k, v = _inproj(x1.reshape(b, n1, n2, D_MODEL), w["g_mix"], w["w_proj"], cos_t, sin_t,
                             twc, tws, d1, n1=n1, n2=n2)
    ya = _attention(q.reshape(t, ATTN_WIDTH), k.reshape(t, KV_WIDTH), v.reshape(t, KV_WIDTH), w["attn_sink"],
                    seq=seq)
    x2 = _mixout(t_arr, x1.reshape(b, n2, n1, D_MODEL), ya.reshape(b, n2, n1, ATTN_WIDTH), d2, w["w_out"],
                 n1=n1, n2=n2)
    y = _ffn(x2.reshape(t, D_MODEL), w["g_ffn2"], w["w2_gate"], w["w2_up"], w["w2_down"], w["g_final"],
             final_norm=True)
    return y.reshape(b, seq, D_MODEL)


def kernel(x_prompt, x_sample, g_ffn1, w1_gate, w1_up, w1_down, g_mix, w_in, w_fourier, attn_sink, w_out,
           g_ffn2, w2_gate, w2_up, w2_down, g_final):
    assert g_ffn1.shape[0] == 1, "single-layer trunk"
    wz = _fold_weights(w_in[0][:, :FOURIER_WIDTH], w_fourier[0])
    w = {
        "g_ffn1": g_ffn1[0][None, :], "g_mix": g_mix[0][None, :], "g_ffn2": g_ffn2[0][None, :],
        "g_final": g_final[None, :],
        "w1_gate": w1_gate[0].astype(BF16), "w1_up": w1_up[0].astype(BF16), "w1_down": w1_down[0].astype(BF16),
        "w2_gate": w2_gate[0].astype(BF16), "w2_up": w2_up[0].astype(BF16), "w2_down": w2_down[0].astype(BF16),
        "w_proj": jnp.concatenate([wz, w_in[0][:, FOURIER_WIDTH:]], axis=1).astype(BF16),
        "w_out": w_out[0].astype(BF16),
        "attn_sink": attn_sink[0],
    }
    y_prompt = _trunk(x_prompt, w, n1=128, n2=128)
    y_sample = _trunk(x_sample, w, n1=64, n2=64)
    return (y_prompt, y_sample)
```

```python
import functools

import numpy as np
import jax
import jax.numpy as jnp
from jax import lax
from jax.experimental import pallas as pl
from jax.experimental.pallas import tpu as pltpu

D_MODEL = 1024
HEAD_DIM = 64
N_FOURIER_GROUPS = 8
FOURIER_WIDTH = N_FOURIER_GROUPS * HEAD_DIM
N_Q_HEADS = 8
N_KV_HEADS = 2
Q_PER_KV = N_Q_HEADS // N_KV_HEADS
ATTN_WIDTH = N_Q_HEADS * HEAD_DIM
KV_WIDTH = N_KV_HEADS * HEAD_DIM
D_FF = 2816
WINDOW = 128
BLOCK = 128
ROPE_THETA = 10000.0
RMS_EPS = 1e-6
NEG_INF = -1e30

SUBLANES = 8
LANES = 128
PROJ_WIDTH = 2 * FOURIER_WIDTH + ATTN_WIDTH + 2 * KV_WIDTH
VMEM_LIMIT_BYTES = 56 * 1024 * 1024
TOKEN_TILE = 512

F32 = jnp.float32
BF16 = jnp.bfloat16


def _rms(x, g):
    return x * lax.rsqrt(jnp.mean(x * x, axis=-1, keepdims=True) + RMS_EPS) * g


def _const_spec(shape):
    zeros = (0,) * len(shape)
    return pl.BlockSpec(shape, lambda *_: zeros, pipeline_mode=pl.Buffered(1))


def _params(n_axes):
    return pltpu.CompilerParams(dimension_semantics=("arbitrary",) * n_axes,
                                vmem_limit_bytes=VMEM_LIMIT_BYTES)


def _fold_kernel(wu_ref, wf_ref, cc_ref, sc_ref, o_ref):
    hi = lax.Precision.HIGHEST
    for g in range(N_FOURIER_GROUPS):
        wf = wf_ref[g]
        pr = jnp.dot(cc_ref[...], wf, precision=hi, preferred_element_type=F32)
        pi = jnp.dot(sc_ref[...], wf, precision=hi, preferred_element_type=F32)
        wug = wu_ref[:, g * HEAD_DIM:(g + 1) * HEAD_DIM]
        o_ref[:, g * HEAD_DIM:(g + 1) * HEAD_DIM] = jnp.dot(wug, pr, precision=hi, preferred_element_type=F32)
        o_ref[:, FOURIER_WIDTH + g * HEAD_DIM:FOURIER_WIDTH + (g + 1) * HEAD_DIM] = -jnp.dot(
            wug, pi, precision=hi, preferred_element_type=F32)


def _fold_weights(w_in_u, w_fourier):
    c = np.arange(HEAD_DIM)
    ang = 2.0 * np.pi * np.outer(c, c) / HEAD_DIM
    scale = HEAD_DIM ** -0.5
    cc = jnp.asarray(np.cos(ang) * scale, F32)
    sc = jnp.asarray(np.sin(ang) * scale, F32)
    return pl.pallas_call(
        _fold_kernel,
        out_shape=jax.ShapeDtypeStruct((D_MODEL, 2 * FOURIER_WIDTH), F32),
        name="fold",
    )(w_in_u, w_fourier, cc, sc)


def _ffn_kernel(x_ref, g_ref, wg_ref, wu_ref, wd_ref, o_ref):
    x = x_ref[...]
    xn = _rms(x, g_ref[...]).astype(BF16)
    gate = jnp.dot(xn, wg_ref[...], preferred_element_type=F32)
    up = jnp.dot(xn, wu_ref[...], preferred_element_type=F32)
    act = (gate * jax.nn.sigmoid(gate) * up).astype(BF16)
    o_ref[...] = x + 0.5 * jnp.dot(act, wd_ref[...], preferred_element_type=F32)


def _ffn(x, g, wg, wu, wd):
    t = x.shape[0]
    tm = TOKEN_TILE
    return pl.pallas_call(
        _ffn_kernel,
        grid=(t // tm,),
        in_specs=[
            pl.BlockSpec((tm, D_MODEL), lambda i: (i, 0)),
            _const_spec((1, D_MODEL)),
            _const_spec((D_MODEL, D_FF)),
            _const_spec((D_MODEL, D_FF)),
            _const_spec((D_FF, D_MODEL)),
        ],
        out_specs=pl.BlockSpec((tm, D_MODEL), lambda i: (i, 0)),
        out_shape=jax.ShapeDtypeStruct((t, D_MODEL), F32),
        compiler_params=_params(1),
        name="ffn",
    )(x, g, wg, wu, wd)


def _inproj_kernel(x_ref, g_ref, w_ref, cos_ref, sin_ref, twc_ref, tws_ref, d1_ref,
                   t_ref, q_ref, k_ref, v_ref, z_scr, *, n1):
    tm = n1 * SUBLANES
    x = x_ref[...].reshape(tm, D_MODEL)
    hb = _rms(x, g_ref[...]).astype(BF16)
    proj = jnp.dot(hb, w_ref[...], preferred_element_type=F32)
    for c in range(2 * FOURIER_WIDTH // LANES):
        z_scr[c] = proj[:, c * LANES:(c + 1) * LANES]

    cos = cos_ref[...].reshape(tm, LANES)
    sin = sin_ref[...].reshape(tm, LANES)
    lane = lax.broadcasted_iota(jnp.int32, (tm, LANES), 1)
    first_half = (lane % HEAD_DIM) < (HEAD_DIM // 2)

    def rope(xc):
        rot = jnp.where(first_half, pltpu.roll(xc, LANES - HEAD_DIM // 2, 1), pltpu.roll(xc, HEAD_DIM // 2, 1))
        return xc * cos + rot * sin

    q0 = 2 * FOURIER_WIDTH
    for c in range(ATTN_WIDTH // LANES):
        qc = rope(proj[:, q0 + c * LANES:q0 + (c + 1) * LANES]) * (HEAD_DIM ** -0.5)
        q_ref[:, :, c * LANES:(c + 1) * LANES] = qc.reshape(n1, SUBLANES, LANES)
    k0 = q0 + ATTN_WIDTH
    k_ref[...] = rope(proj[:, k0:k0 + KV_WIDTH]).reshape(n1, SUBLANES, KV_WIDTH)
    v_ref[...] = proj[:, k0 + KV_WIDTH:k0 + 2 * KV_WIDTH].reshape(n1, SUBLANES, KV_WIDTH)

    d1 = d1_ref[...]
    for j in range(SUBLANES):
        rows = pl.ds(j, n1, stride=SUBLANES)
        nch = FOURIER_WIDTH // LANES
        zr = jnp.concatenate([z_scr[c, rows, :] for c in range(nch)], axis=1)
        zi = jnp.concatenate([z_scr[nch + c, rows, :] for c in range(nch)], axis=1)
        st = jnp.concatenate([zr, zi], axis=0).astype(BF16)
        t = jnp.dot(d1, st, preferred_element_type=F32)
        tr, ti = t[:n1], t[n1:]
        c4 = jnp.concatenate([twc_ref[j]] * (FOURIER_WIDTH // LANES), axis=1)
        s4 = jnp.concatenate([tws_ref[j]] * (FOURIER_WIDTH // LANES), axis=1)
        t_ref[j, :n1, :] = tr * c4 + ti * s4
        t_ref[j, n1:, :] = ti * c4 - tr * s4


def _inproj(x1, g, w, cos_t, sin_t, twc, tws, d1, *, n1, n2):
    b = x1.shape[0]
    blk = lambda w_: pl.BlockSpec((None, n1, SUBLANES, w_), lambda bi, i: (bi, 0, i, 0))
    tab = pl.BlockSpec((n1, SUBLANES, LANES), lambda bi, i: (0, i, 0))
    tw = pl.BlockSpec((SUBLANES, n1, LANES), lambda bi, i: (i, 0, 0))
    return pl.pallas_call(
        functools.partial(_inproj_kernel, n1=n1),
        grid=(b, n2 // SUBLANES),
        in_specs=[blk(D_MODEL), _const_spec((1, D_MODEL)), _const_spec((D_MODEL, PROJ_WIDTH)),
                  tab, tab, tw, tw, _const_spec((2 * n1, 2 * n1))],
        out_specs=[
            pl.BlockSpec((None, SUBLANES, 2 * n1, FOURIER_WIDTH), lambda bi, i: (bi, i, 0, 0)),
            blk(ATTN_WIDTH), blk(KV_WIDTH), blk(KV_WIDTH),
        ],
        out_shape=[
            jax.ShapeDtypeStruct((b, n2, 2 * n1, FOURIER_WIDTH), F32),
            jax.ShapeDtypeStruct((b, n1, n2, ATTN_WIDTH), F32),
            jax.ShapeDtypeStruct((b, n1, n2, KV_WIDTH), F32),
            jax.ShapeDtypeStruct((b, n1, n2, KV_WIDTH), F32),
        ],
        scratch_shapes=[pltpu.VMEM((2 * FOURIER_WIDTH // LANES, n1 * SUBLANES, LANES), F32)],
        compiler_params=_params(2),
        name="inproj",
    )(x1, g, w, cos_t, sin_t, twc, tws, d1)


def _attn_kernel(sink_ref, q_ref, kp_ref, kc_ref, kn_ref, vp_ref, vc_ref, vn_ref, o_ref, *, nb, nblk_seq):
    i = pl.program_id(0)
    rows = (nb + 2) * BLOCK
    kfull = jnp.concatenate([kp_ref[...], kc_ref[...], kn_ref[...]], axis=0)
    vfull = jnp.concatenate([vp_ref[...], vc_ref[...], vn_ref[...]], axis=0)
    lo = lax.broadcasted_iota(jnp.int32, (rows, LANES), 1) < HEAD_DIM

    def dup(x):
        r = pltpu.roll(x, HEAD_DIM, 1)
        return jnp.where(lo, x, r), jnp.where(lo, r, x)

    kk = dup(kfull)
    vv = dup(vfull)
    wkeys = 3 * BLOCK
    grp = lax.broadcasted_iota(jnp.int32, (wkeys, Q_PER_KV * HEAD_DIM), 1) // HEAD_DIM
    sidx = lax.broadcasted_iota(jnp.int32, (BLOCK, wkeys), 1)
    qidx = lax.broadcasted_iota(jnp.int32, (BLOCK, wkeys), 0)
    band = jnp.abs(sidx - BLOCK - qidx) <= WINDOW

    def blockdiag(x):
        wide = jnp.concatenate([x, x], axis=1)
        return jnp.concatenate([jnp.where(grp == g, wide, 0.0) for g in range(Q_PER_KV)], axis=0).astype(BF16)

    for n in range(nb):
        pb = (i * nb + n) % nblk_seq
        s_lo = jnp.where(pb > 0, 0, BLOCK)
        s_hi = jnp.where(pb < nblk_seq - 1, wkeys, 2 * BLOCK)
        mask = band & (sidx >= s_lo) & (sidx < s_hi)
        for kh in range(N_KV_HEADS):
            kbd = blockdiag(kk[kh][n * BLOCK:n * BLOCK + wkeys])
            vbd = blockdiag(vv[kh][n * BLOCK:n * BLOCK + wkeys])
            c0 = kh * Q_PER_KV * HEAD_DIM
            qh = q_ref[n * BLOCK:(n + 1) * BLOCK, c0:c0 + Q_PER_KV * HEAD_DIM].astype(BF16)
            s_all = lax.dot_general(qh, kbd, (((1,), (1,)), ((), ())), preferred_element_type=F32)
            ps = []
            for g in range(Q_PER_KV):
                sg = jnp.where(mask, s_all[:, g * wkeys:(g + 1) * wkeys], NEG_INF)
                sink = sink_ref[kh * Q_PER_KV + g]
                m = jnp.maximum(jnp.max(sg, axis=-1, keepdims=True), sink)
                p = jnp.exp(sg - m)
                den = jnp.sum(p, axis=-1, keepdims=True) + jnp.exp(sink - m)
                ps.append((p / den).astype(BF16))
            p_all = jnp.concatenate(ps, axis=1)
            o_ref[n * BLOCK:(n + 1) * BLOCK, c0:c0 + Q_PER_KV * HEAD_DIM] = jnp.dot(
                p_all, vbd, preferred_element_type=F32)


def _attention(q, k, v, sink, *, seq, nb=4):
    t = q.shape[0]
    nblk = t // BLOCK
    tq = nb * BLOCK
    cur = lambda w_: pl.BlockSpec((tq, w_), lambda i: (i, 0))
    prev = pl.BlockSpec((BLOCK, KV_WIDTH), lambda i: (jnp.maximum(i * nb - 1, 0), 0))
    nxt = pl.BlockSpec((BLOCK, KV_WIDTH), lambda i: (jnp.minimum(i * nb + nb, nblk - 1), 0))
    return pl.pallas_call(
        functools.partial(_attn_kernel, nb=nb, nblk_seq=seq // BLOCK),
        grid=(t // tq,),
        in_specs=[pl.BlockSpec(memory_space=pltpu.SMEM), cur(ATTN_WIDTH),
                  prev, cur(KV_WIDTH), nxt, prev, cur(KV_WIDTH), nxt],
        out_specs=cur(ATTN_WIDTH),
        out_shape=jax.ShapeDtypeStruct((t, ATTN_WIDTH), F32),
        compiler_params=_params(1),
        name="attention",
    )(sink, q, k, k, k, v, v, v)


FF_CHUNKS = ((0, 1024), (1024, 2048), (2048, D_FF))


def _mixffn_kernel(tr_ref, ti_ref, x_ref, ya_ref, d2_ref, wo_ref, g2_ref, wg_ref, wu_ref, wd_ref, gf_ref,
                   o_ref, tr_scr, ti_scr, yf_scr, *, n2, n2h):
    tm = n2h * SUBLANES
    nch = FOURIER_WIDTH // LANES

    @pl.when(pl.program_id(2) == 0)
    def _():
        for c in range(nch):
            tr_scr[c] = tr_ref[:, :, c * LANES:(c + 1) * LANES].reshape(n2 * SUBLANES, LANES)
            ti_scr[c] = ti_ref[:, :, c * LANES:(c + 1) * LANES].reshape(n2 * SUBLANES, LANES)

    d2 = d2_ref[...]
    for j in range(SUBLANES):
        rows = pl.ds(j, n2, stride=SUBLANES)
        tr = jnp.concatenate([tr_scr[c, rows, :] for c in range(nch)], axis=1)
        ti = jnp.concatenate([ti_scr[c, rows, :] for c in range(nch)], axis=1)
        st = jnp.concatenate([tr, ti], axis=0).astype(BF16)
        yfj = jnp.dot(d2, st, preferred_element_type=F32)
        for c in range(nch):
            yf_scr[c, pl.ds(j, n2h, stride=SUBLANES), :] = yfj[:, c * LANES:(c + 1) * LANES]
    yf = jnp.concatenate([yf_scr[c] for c in range(nch)], axis=1).astype(BF16)
    ya = ya_ref[...].reshape(tm, ATTN_WIDTH).astype(BF16)
    x2 = (x_ref[...].reshape(tm, D_MODEL)
          + jnp.dot(yf, wo_ref[:FOURIER_WIDTH, :], preferred_element_type=F32)
          + jnp.dot(ya, wo_ref[FOURIER_WIDTH:, :], preferred_element_type=F32))

    xn = _rms(x2, g2_ref[...]).astype(BF16)
    acc = None
    for lo, hi in FF_CHUNKS:
        gate = jnp.dot(xn, wg_ref[:, lo:hi], preferred_element_type=F32)
        up = jnp.dot(xn, wu_ref[:, lo:hi], preferred_element_type=F32)
        act = (gate * jax.nn.sigmoid(gate) * up).astype(BF16)
        part = jnp.dot(act, wd_ref[lo:hi, :], preferred_element_type=F32)
        acc = part if acc is None else acc + part
    y = _rms(x2 + 0.5 * acc, gf_ref[...])
    o_ref[...] = y.reshape(n2h, SUBLANES, D_MODEL)


def _mixffn(t_arr, x1, ya, d2, wo, g2, wg, wu, wd, gf, *, n1, n2):
    b = x1.shape[0]
    halves = (n2 * SUBLANES) // TOKEN_TILE
    n2h = n2 // halves
    t_re = pl.BlockSpec((None, n2, SUBLANES, FOURIER_WIDTH), lambda bi, i, h: (bi, 0, i, 0))
    t_im = pl.BlockSpec((None, n2, SUBLANES, FOURIER_WIDTH), lambda bi, i, h: (bi, 0, n1 // SUBLANES + i, 0))
    blk = lambda w_: pl.BlockSpec((None, n2h, SUBLANES, w_), lambda bi, i, h: (bi, h, i, 0))
    return pl.pallas_call(
        functools.partial(_mixffn_kernel, n2=n2, n2h=n2h),
        grid=(b, n1 // SUBLANES, halves),
        in_specs=[t_re, t_im, blk(D_MODEL), blk(ATTN_WIDTH),
                  pl.BlockSpec((n2h, 2 * n2), lambda bi, i, h: (h, 0)),
                  _const_spec((2 * FOURIER_WIDTH, D_MODEL)), _const_spec((1, D_MODEL)),
                  _const_spec((D_MODEL, D_FF)), _const_spec((D_MODEL, D_FF)), _const_spec((D_FF, D_MODEL)),
                  _const_spec((1, D_MODEL))],
        out_specs=blk(D_MODEL),
        out_shape=jax.ShapeDtypeStruct((b, n2, n1, D_MODEL), F32),
        scratch_shapes=[pltpu.VMEM((FOURIER_WIDTH // LANES, n2 * SUBLANES, LANES), F32)] * 2
        + [pltpu.VMEM((FOURIER_WIDTH // LANES, n2h * SUBLANES, LANES), F32)],
        compiler_params=_params(3),
        name="mixffn",
    )(t_arr, t_arr, x1, ya, d2, wo, g2, wg, wu, wd, gf)


def _dft_tables(n1, n2):
    s = n1 * n2
    a1 = 2.0 * np.pi * np.outer(np.arange(n1), np.arange(n1)) / n1
    c1, s1 = np.cos(a1) / np.sqrt(n1), np.sin(a1) / np.sqrt(n1)
    d1 = np.block([[c1, s1], [-s1, c1]])
    a2 = 2.0 * np.pi * np.outer(np.arange(n2), np.arange(n2)) / n2
    d2 = np.concatenate([np.cos(a2), np.sin(a2)], axis=1) / np.sqrt(n2)
    m = (jnp.arange(n2, dtype=jnp.int32)[:, None] * jnp.arange(n1, dtype=jnp.int32)[None, :]) % s
    ang = m.astype(F32) * (2.0 * np.pi / s)
    twc = jnp.broadcast_to(jnp.cos(ang)[:, :, None], (n2, n1, LANES))
    tws = jnp.broadcast_to(jnp.sin(ang)[:, :, None], (n2, n1, LANES))
    return jnp.asarray(d1, F32).astype(BF16), jnp.asarray(d2, F32).astype(BF16), twc, tws


def _rope_tables(seq, n1, n2):
    half = HEAD_DIM // 2
    inv_freq = ROPE_THETA ** (-jnp.arange(half, dtype=F32) / half)
    ang = jnp.arange(seq).astype(F32)[:, None] * inv_freq[None, :]
    cos, sin = jnp.cos(ang), jnp.sin(ang)
    reps = LANES // HEAD_DIM
    cos_t = jnp.tile(jnp.concatenate([cos, cos], axis=1), (1, reps))
    sin_t = jnp.tile(jnp.concatenate([-sin, sin], axis=1), (1, reps))
    return cos_t.reshape(n1, n2, LANES), sin_t.reshape(n1, n2, LANES)


def _trunk(x, w, *, n1, n2):
    b, seq, _ = x.shape
    t = b * seq
    d1, d2, twc, tws = _dft_tables(n1, n2)
    cos_t, sin_t = _rope_tables(seq, n1, n2)
    x1 = _ffn(x.reshape(t, D_MODEL), w["g_ffn1"], w["w1_gate"], w["w1_up"], w["w1_down"])
    t_arr, q, k, v = _inproj(x1.reshape(b, n1, n2, D_MODEL), w["g_mix"], w["w_proj"], cos_t, sin_t,
                             twc, tws, d1, n1=n1, n2=n2)
    ya = _attention(q.reshape(t, ATTN_WIDTH), k.reshape(t, KV_WIDTH), v.reshape(t, KV_WIDTH), w["attn_sink"],
                    seq=seq)
    y = _mixffn(t_arr, x1.reshape(b, n2, n1, D_MODEL), ya.reshape(b, n2, n1, ATTN_WIDTH), d2, w["w_out"],
                w["g_ffn2"], w["w2_gate"], w["w2_up"], w["w2_down"], w["g_final"], n1=n1, n2=n2)
    return y.reshape(b, seq, D_MODEL)


def kernel(x_prompt, x_sample, g_ffn1, w1_gate, w1_up, w1_down, g_mix, w_in, w_fourier, attn_sink, w_out,
           g_ffn2, w2_gate, w2_up, w2_down, g_final):
    assert g_ffn1.shape[0] == 1, "single-layer trunk"
    wz = _fold_weights(w_in[0][:, :FOURIER_WIDTH], w_fourier[0])
    w = {
        "g_ffn1": g_ffn1[0][None, :], "g_mix": g_mix[0][None, :], "g_ffn2": g_ffn2[0][None, :],
        "g_final": g_final[None, :],
        "w1_gate": w1_gate[0].astype(BF16), "w1_up": w1_up[0].astype(BF16), "w1_down": w1_down[0].astype(BF16),
        "w2_gate": w2_gate[0].astype(BF16), "w2_up": w2_up[0].astype(BF16), "w2_down": w2_down[0].astype(BF16),
        "w_proj": jnp.concatenate([wz, w_in[0][:, FOURIER_WIDTH:]], axis=1).astype(BF16),
        "w_out": w_out[0].astype(BF16),
        "attn_sink": attn_sink[0],
    }
    y_prompt = _trunk(x_prompt, w, n1=128, n2=128)
    y_sample = _trunk(x_sample, w, n1=64, n2=64)
    return (y_prompt, y_sample)
```

```python
import functools

import numpy as np
import jax
import jax.numpy as jnp
from jax import lax
from jax.experimental import pallas as pl
from jax.experimental.pallas import tpu as pltpu

D_MODEL = 1024
HEAD_DIM = 64
N_FOURIER_GROUPS = 8
FOURIER_WIDTH = N_FOURIER_GROUPS * HEAD_DIM
N_Q_HEADS = 8
N_KV_HEADS = 2
Q_PER_KV = N_Q_HEADS // N_KV_HEADS
ATTN_WIDTH = N_Q_HEADS * HEAD_DIM
KV_WIDTH = N_KV_HEADS * HEAD_DIM
D_FF = 2816
WINDOW = 128
BLOCK = 128
ROPE_THETA = 10000.0
RMS_EPS = 1e-6
NEG_INF = -1e30

SUBLANES = 8
LANES = 128
PROJ_WIDTH = 2 * FOURIER_WIDTH + ATTN_WIDTH + 2 * KV_WIDTH
VMEM_LIMIT_BYTES = 56 * 1024 * 1024
TOKEN_TILE = 512

F32 = jnp.float32
BF16 = jnp.bfloat16


def _rms(x, g):
    return x * lax.rsqrt(jnp.mean(x * x, axis=-1, keepdims=True) + RMS_EPS) * g


def _const_spec(shape):
    zeros = (0,) * len(shape)
    return pl.BlockSpec(shape, lambda *_: zeros, pipeline_mode=pl.Buffered(1))


def _params(n_axes):
    return pltpu.CompilerParams(dimension_semantics=("arbitrary",) * n_axes,
                                vmem_limit_bytes=VMEM_LIMIT_BYTES)


def _fold_kernel(wu_ref, wf_ref, cc_ref, sc_ref, o_ref):
    hi = lax.Precision.HIGHEST
    for g in range(N_FOURIER_GROUPS):
        wf = wf_ref[g]
        pr = jnp.dot(cc_ref[...], wf, precision=hi, preferred_element_type=F32)
        pi = jnp.dot(sc_ref[...], wf, precision=hi, preferred_element_type=F32)
        wug = wu_ref[:, g * HEAD_DIM:(g + 1) * HEAD_DIM]
        o_ref[:, g * HEAD_DIM:(g + 1) * HEAD_DIM] = jnp.dot(wug, pr, precision=hi, preferred_element_type=F32)
        o_ref[:, FOURIER_WIDTH + g * HEAD_DIM:FOURIER_WIDTH + (g + 1) * HEAD_DIM] = -jnp.dot(
            wug, pi, precision=hi, preferred_element_type=F32)


def _fold_weights(w_in_u, w_fourier):
    c = np.arange(HEAD_DIM)
    ang = 2.0 * np.pi * np.outer(c, c) / HEAD_DIM
    scale = HEAD_DIM ** -0.5
    cc = jnp.asarray(np.cos(ang) * scale, F32)
    sc = jnp.asarray(np.sin(ang) * scale, F32)
    return pl.pallas_call(
        _fold_kernel,
        out_shape=jax.ShapeDtypeStruct((D_MODEL, 2 * FOURIER_WIDTH), F32),
        name="fold",
    )(w_in_u, w_fourier, cc, sc)


def _ffn_kernel(x_ref, g_ref, wg_ref, wu_ref, wd_ref, o_ref):
    x = x_ref[...]
    xn = _rms(x, g_ref[...]).astype(BF16)
    gate = jnp.dot(xn, wg_ref[...], preferred_element_type=F32)
    up = jnp.dot(xn, wu_ref[...], preferred_element_type=F32)
    act = (gate * jax.nn.sigmoid(gate) * up).astype(BF16)
    o_ref[...] = x + 0.5 * jnp.dot(act, wd_ref[...], preferred_element_type=F32)


def _ffn(x, g, wg, wu, wd):
    t = x.shape[0]
    tm = TOKEN_TILE
    return pl.pallas_call(
        _ffn_kernel,
        grid=(t // tm,),
        in_specs=[
            pl.BlockSpec((tm, D_MODEL), lambda i: (i, 0)),
            _const_spec((1, D_MODEL)),
            _const_spec((D_MODEL, D_FF)),
            _const_spec((D_MODEL, D_FF)),
            _const_spec((D_FF, D_MODEL)),
        ],
        out_specs=pl.BlockSpec((tm, D_MODEL), lambda i: (i, 0)),
        out_shape=jax.ShapeDtypeStruct((t, D_MODEL), F32),
        compiler_params=_params(1),
        name="ffn",
    )(x, g, wg, wu, wd)


def _inproj_kernel(x_ref, g_ref, w_ref, cos_ref, sin_ref, twc_ref, tws_ref, d1_ref,
                   t_ref, q_ref, k_ref, v_ref, z_scr, *, n1):
    tm = n1 * SUBLANES
    x = x_ref[...].reshape(tm, D_MODEL)
    hb = _rms(x, g_ref[...]).astype(BF16)
    proj = jnp.dot(hb, w_ref[...], preferred_element_type=F32)
    for c in range(2 * FOURIER_WIDTH // LANES):
        z_scr[c] = proj[:, c * LANES:(c + 1) * LANES]

    cos = cos_ref[...].reshape(tm, LANES)
    sin = sin_ref[...].reshape(tm, LANES)
    lane = lax.broadcasted_iota(jnp.int32, (tm, LANES), 1)
    first_half = (lane % HEAD_DIM) < (HEAD_DIM // 2)

    def rope(xc):
        rot = jnp.where(first_half, pltpu.roll(xc, LANES - HEAD_DIM // 2, 1), pltpu.roll(xc, HEAD_DIM // 2, 1))
        return xc * cos + rot * sin

    q0 = 2 * FOURIER_WIDTH
    for c in range(ATTN_WIDTH // LANES):
        qc = rope(proj[:, q0 + c * LANES:q0 + (c + 1) * LANES]) * (HEAD_DIM ** -0.5)
        q_ref[:, :, c * LANES:(c + 1) * LANES] = qc.reshape(n1, SUBLANES, LANES)
    k0 = q0 + ATTN_WIDTH
    k_ref[...] = rope(proj[:, k0:k0 + KV_WIDTH]).reshape(n1, SUBLANES, KV_WIDTH)
    v_ref[...] = proj[:, k0 + KV_WIDTH:k0 + 2 * KV_WIDTH].reshape(n1, SUBLANES, KV_WIDTH)

    d1 = d1_ref[...]
    for j in range(SUBLANES):
        rows = pl.ds(j, n1, stride=SUBLANES)
        nch = FOURIER_WIDTH // LANES
        zr = jnp.concatenate([z_scr[c, rows, :] for c in range(nch)], axis=1)
        zi = jnp.concatenate([z_scr[nch + c, rows, :] for c in range(nch)], axis=1)
        st = jnp.concatenate([zr, zi], axis=0).astype(BF16)
        t = jnp.dot(d1, st, preferred_element_type=F32)
        tr, ti = t[:n1], t[n1:]
        c4 = jnp.concatenate([twc_ref[j]] * (FOURIER_WIDTH // LANES), axis=1)
        s4 = jnp.concatenate([tws_ref[j]] * (FOURIER_WIDTH // LANES), axis=1)
        t_ref[j, :n1, :] = tr * c4 + ti * s4
        t_ref[j, n1:, :] = ti * c4 - tr * s4


def _inproj(x1, g, w, cos_t, sin_t, twc, tws, d1, *, n1, n2):
    b = x1.shape[0]
    blk = lambda w_: pl.BlockSpec((None, n1, SUBLANES, w_), lambda bi, i: (bi, 0, i, 0))
    tab = pl.BlockSpec((n1, SUBLANES, LANES), lambda bi, i: (0, i, 0))
    tw = pl.BlockSpec((SUBLANES, n1, LANES), lambda bi, i: (i, 0, 0))
    return pl.pallas_call(
        functools.partial(_inproj_kernel, n1=n1),
        grid=(b, n2 // SUBLANES),
        in_specs=[blk(D_MODEL), _const_spec((1, D_MODEL)), _const_spec((D_MODEL, PROJ_WIDTH)),
                  tab, tab, tw, tw, _const_spec((2 * n1, 2 * n1))],
        out_specs=[
            pl.BlockSpec((None, SUBLANES, 2 * n1, FOURIER_WIDTH), lambda bi, i: (bi, i, 0, 0)),
            blk(ATTN_WIDTH), blk(KV_WIDTH), blk(KV_WIDTH),
        ],
        out_shape=[
            jax.ShapeDtypeStruct((b, n2, 2 * n1, FOURIER_WIDTH), F32),
            jax.ShapeDtypeStruct((b, n1, n2, ATTN_WIDTH), F32),
            jax.ShapeDtypeStruct((b, n1, n2, KV_WIDTH), F32),
            jax.ShapeDtypeStruct((b, n1, n2, KV_WIDTH), F32),
        ],
        scratch_shapes=[pltpu.VMEM((2 * FOURIER_WIDTH // LANES, n1 * SUBLANES, LANES), F32)],
        compiler_params=_params(2),
        name="inproj",
    )(x1, g, w, cos_t, sin_t, twc, tws, d1)


F32_MAX = float(np.finfo(np.float32).max)
GROUP_WIDTH = Q_PER_KV * HEAD_DIM


def _attn_kernel(sink_ref, q_ref, kp_ref, kc_ref, kn_ref, vp_ref, vc_ref, vn_ref, o_ref, *, nb, nblk_seq):
    i = pl.program_id(0)
    rows = (nb + 2) * BLOCK
    kfull = jnp.concatenate([kp_ref[...], kc_ref[...], kn_ref[...]], axis=0)
    vfull = jnp.concatenate([vp_ref[...], vc_ref[...], vn_ref[...]], axis=0)
    lo = lax.broadcasted_iota(jnp.int32, (rows, LANES), 1) < HEAD_DIM

    def dup(x):
        r = pltpu.roll(x, HEAD_DIM, 1)
        return jnp.where(lo, x, r), jnp.where(lo, r, x)

    kk = dup(kfull)
    vv = dup(vfull)
    wkeys = 3 * BLOCK
    grp = lax.broadcasted_iota(jnp.int32, (wkeys, GROUP_WIDTH), 1) // HEAD_DIM
    ogrp = lax.broadcasted_iota(jnp.int32, (BLOCK, GROUP_WIDTH), 1) // HEAD_DIM
    kidx = lax.broadcasted_iota(jnp.int32, (BLOCK, BLOCK), 1)
    qidx = lax.broadcasted_iota(jnp.int32, (BLOCK, BLOCK), 0)
    cap_first = jnp.where(kidx >= qidx, F32_MAX, NEG_INF)
    cap_last = jnp.where(kidx <= qidx, F32_MAX, NEG_INF)

    def blockdiag(x):
        wide = jnp.concatenate([x, x], axis=1)
        return jnp.concatenate([jnp.where(grp == g, wide, 0.0) for g in range(Q_PER_KV)], axis=0).astype(BF16)

    for n in range(nb):
        pb = (i * nb + n) % nblk_seq
        capf = jnp.minimum(cap_first, jnp.where(pb > 0, F32_MAX, NEG_INF))
        capl = jnp.minimum(cap_last, jnp.where(pb < nblk_seq - 1, F32_MAX, NEG_INF))
        for kh in range(N_KV_HEADS):
            kbd = blockdiag(kk[kh][n * BLOCK:n * BLOCK + wkeys])
            vbd = blockdiag(vv[kh][n * BLOCK:n * BLOCK + wkeys])
            c0 = kh * GROUP_WIDTH
            qh = q_ref[n * BLOCK:(n + 1) * BLOCK, c0:c0 + GROUP_WIDTH].astype(BF16)
            s_all = lax.dot_general(qh, kbd, (((1,), (1,)), ((), ())), preferred_element_type=F32)
            ps = []
            scale = None
            for g in range(Q_PER_KV - 1, -1, -1):
                s0 = jnp.minimum(s_all[:, g * wkeys:g * wkeys + BLOCK], capf)
                s1 = s_all[:, g * wkeys + BLOCK:g * wkeys + 2 * BLOCK]
                s2 = jnp.minimum(s_all[:, g * wkeys + 2 * BLOCK:(g + 1) * wkeys], capl)
                sink = sink_ref[kh * Q_PER_KV + g]
                m = jnp.max(jnp.maximum(jnp.maximum(s0, s1), s2), axis=-1, keepdims=True)
                m = jnp.maximum(m, sink)
                p = [jnp.exp((t - m).astype(BF16)) for t in (s0, s1, s2)]
                psum = (p[0] + p[1] + p[2]).astype(F32)
                den = jnp.sum(psum, axis=-1, keepdims=True) + jnp.exp(sink - m)
                rinv = 1.0 / den
                scale = rinv if scale is None else jnp.where(ogrp == g, rinv, scale)
                ps = p + ps
            p_all = jnp.concatenate(ps, axis=1)
            o_ref[n * BLOCK:(n + 1) * BLOCK, c0:c0 + GROUP_WIDTH] = jnp.dot(
                p_all, vbd, preferred_element_type=F32) * scale


def _attention(q, k, v, sink, *, seq, nb=4):
    t = q.shape[0]
    nblk = t // BLOCK
    tq = nb * BLOCK
    cur = lambda w_: pl.BlockSpec((tq, w_), lambda i: (i, 0))
    prev = pl.BlockSpec((BLOCK, KV_WIDTH), lambda i: (jnp.maximum(i * nb - 1, 0), 0))
    nxt = pl.BlockSpec((BLOCK, KV_WIDTH), lambda i: (jnp.minimum(i * nb + nb, nblk - 1), 0))
    return pl.pallas_call(
        functools.partial(_attn_kernel, nb=nb, nblk_seq=seq // BLOCK),
        grid=(t // tq,),
        in_specs=[pl.BlockSpec(memory_space=pltpu.SMEM), cur(ATTN_WIDTH),
                  prev, cur(KV_WIDTH), nxt, prev, cur(KV_WIDTH), nxt],
        out_specs=cur(ATTN_WIDTH),
        out_shape=jax.ShapeDtypeStruct((t, ATTN_WIDTH), F32),
        compiler_params=_params(1),
        name="attention",
    )(sink, q, k, k, k, v, v, v)


FF_CHUNKS = ((0, 1024), (1024, 2048), (2048, D_FF))


def _mixffn_kernel(tr_ref, ti_ref, x_ref, ya_ref, d2_ref, wo_ref, g2_ref, wg_ref, wu_ref, wd_ref, gf_ref,
                   o_ref, tr_scr, ti_scr, yf_scr, *, n2, n2h):
    tm = n2h * SUBLANES
    nch = FOURIER_WIDTH // LANES

    @pl.when(pl.program_id(2) == 0)
    def _():
        for c in range(nch):
            tr_scr[c] = tr_ref[:, :, c * LANES:(c + 1) * LANES].reshape(n2 * SUBLANES, LANES)
            ti_scr[c] = ti_ref[:, :, c * LANES:(c + 1) * LANES].reshape(n2 * SUBLANES, LANES)

    d2 = d2_ref[...]
    for j in range(SUBLANES):
        rows = pl.ds(j, n2, stride=SUBLANES)
        tr = jnp.concatenate([tr_scr[c, rows, :] for c in range(nch)], axis=1)
        ti = jnp.concatenate([ti_scr[c, rows, :] for c in range(nch)], axis=1)
        st = jnp.concatenate([tr, ti], axis=0).astype(BF16)
        yfj = jnp.dot(d2, st, preferred_element_type=F32)
        for c in range(nch):
            yf_scr[c, pl.ds(j, n2h, stride=SUBLANES), :] = yfj[:, c * LANES:(c + 1) * LANES]
    yf = jnp.concatenate([yf_scr[c] for c in range(nch)], axis=1).astype(BF16)
    ya = ya_ref[...].reshape(tm, ATTN_WIDTH).astype(BF16)
    x2 = (x_ref[...].reshape(tm, D_MODEL)
          + jnp.dot(yf, wo_ref[:FOURIER_WIDTH, :], preferred_element_type=F32)
          + jnp.dot(ya, wo_ref[FOURIER_WIDTH:, :], preferred_element_type=F32))

    xn = _rms(x2, g2_ref[...]).astype(BF16)
    acc = None
    for lo, hi in FF_CHUNKS:
        gate = jnp.dot(xn, wg_ref[:, lo:hi], preferred_element_type=F32)
        up = jnp.dot(xn, wu_ref[:, lo:hi], preferred_element_type=F32)
        act = (gate * jax.nn.sigmoid(gate) * up).astype(BF16)
        part = jnp.dot(act, wd_ref[lo:hi, :], preferred_element_type=F32)
        acc = part if acc is None else acc + part
    y = _rms(x2 + 0.5 * acc, gf_ref[...])
    o_ref[...] = y.reshape(n2h, SUBLANES, D_MODEL)


def _mixffn(t_arr, x1, ya, d2, wo, g2, wg, wu, wd, gf, *, n1, n2):
    b = x1.shape[0]
    halves = (n2 * SUBLANES) // TOKEN_TILE
    n2h = n2 // halves
    t_re = pl.BlockSpec((None, n2, SUBLANES, FOURIER_WIDTH), lambda bi, i, h: (bi, 0, i, 0))
    t_im = pl.BlockSpec((None, n2, SUBLANES, FOURIER_WIDTH), lambda bi, i, h: (bi, 0, n1 // SUBLANES + i, 0))
    blk = lambda w_: pl.BlockSpec((None, n2h, SUBLANES, w_), lambda bi, i, h: (bi, h, i, 0))
    return pl.pallas_call(
        functools.partial(_mixffn_kernel, n2=n2, n2h=n2h),
        grid=(b, n1 // SUBLANES, halves),
        in_specs=[t_re, t_im, blk(D_MODEL), blk(ATTN_WIDTH),
                  pl.BlockSpec((n2h, 2 * n2), lambda bi, i, h: (h, 0)),
                  _const_spec((2 * FOURIER_WIDTH, D_MODEL)), _const_spec((1, D_MODEL)),
                  _const_spec((D_MODEL, D_FF)), _const_spec((D_MODEL, D_FF)), _const_spec((D_FF, D_MODEL)),
                  _const_spec((1, D_MODEL))],
        out_specs=blk(D_MODEL),
        out_shape=jax.ShapeDtypeStruct((b, n2, n1, D_MODEL), F32),
        scratch_shapes=[pltpu.VMEM((FOURIER_WIDTH // LANES, n2 * SUBLANES, LANES), F32)] * 2
        + [pltpu.VMEM((FOURIER_WIDTH // LANES, n2h * SUBLANES, LANES), F32)],
        compiler_params=_params(3),
        name="mixffn",
    )(t_arr, t_arr, x1, ya, d2, wo, g2, wg, wu, wd, gf)


def _dft_tables(n1, n2):
    s = n1 * n2
    a1 = 2.0 * np.pi * np.outer(np.arange(n1), np.arange(n1)) / n1
    c1, s1 = np.cos(a1) / np.sqrt(n1), np.sin(a1) / np.sqrt(n1)
    d1 = np.block([[c1, s1], [-s1, c1]])
    a2 = 2.0 * np.pi * np.outer(np.arange(n2), np.arange(n2)) / n2
    d2 = np.concatenate([np.cos(a2), np.sin(a2)], axis=1) / np.sqrt(n2)
    m = (jnp.arange(n2, dtype=jnp.int32)[:, None] * jnp.arange(n1, dtype=jnp.int32)[None, :]) % s
    ang = m.astype(F32) * (2.0 * np.pi / s)
    twc = jnp.broadcast_to(jnp.cos(ang)[:, :, None], (n2, n1, LANES))
    tws = jnp.broadcast_to(jnp.sin(ang)[:, :, None], (n2, n1, LANES))
    return jnp.asarray(d1, F32).astype(BF16), jnp.asarray(d2, F32).astype(BF16), twc, tws


def _rope_tables(seq, n1, n2):
    half = HEAD_DIM // 2
    inv_freq = ROPE_THETA ** (-jnp.arange(half, dtype=F32) / half)
    ang = jnp.arange(seq).astype(F32)[:, None] * inv_freq[None, :]
    cos, sin = jnp.cos(ang), jnp.sin(ang)
    reps = LANES // HEAD_DIM
    cos_t = jnp.tile(jnp.concatenate([cos, cos], axis=1), (1, reps))
    sin_t = jnp.tile(jnp.concatenate([-sin, sin], axis=1), (1, reps))
    return cos_t.reshape(n1, n2, LANES), sin_t.reshape(n1, n2, LANES)


def _trunk(x, w, *, n1, n2):
    b, seq, _ = x.shape
    t = b * seq
    d1, d2, twc, tws = _dft_tables(n1, n2)
    cos_t, sin_t = _rope_tables(seq, n1, n2)
    x1 = _ffn(x.reshape(t, D_MODEL), w["g_ffn1"], w["w1_gate"], w["w1_up"], w["w1_down"])
    t_arr, q, k, v = _inproj(x1.reshape(b, n1, n2, D_MODEL), w["g_mix"], w["w_proj"], cos_t, sin_t,
                             twc, tws, d1, n1=n1, n2=n2)
    ya = _attention(q.reshape(t, ATTN_WIDTH), k.reshape(t, KV_WIDTH), v.reshape(t, KV_WIDTH), w["attn_sink"],
                    seq=seq)
    y = _mixffn(t_arr, x1.reshape(b, n2, n1, D_MODEL), ya.reshape(b, n2, n1, ATTN_WIDTH), d2, w["w_out"],
                w["g_ffn2"], w["w2_gate"], w["w2_up"], w["w2_down"], w["g_final"], n1=n1, n2=n2)
    return y.reshape(b, seq, D_MODEL)


def kernel(x_prompt, x_sample, g_ffn1, w1_gate, w1_up, w1_down, g_mix, w_in, w_fourier, attn_sink, w_out,
           g_ffn2, w2_gate, w2_up, w2_down, g_final):
    assert g_ffn1.shape[0] == 1, "single-layer trunk"
    wz = _fold_weights(w_in[0][:, :FOURIER_WIDTH], w_fourier[0])
    w = {
        "g_ffn1": g_ffn1[0][None, :], "g_mix": g_mix[0][None, :], "g_ffn2": g_ffn2[0][None, :],
        "g_final": g_final[None, :],
        "w1_gate": w1_gate[0].astype(BF16), "w1_up": w1_up[0].astype(BF16), "w1_down": w1_down[0].astype(BF16),
        "w2_gate": w2_gate[0].astype(BF16), "w2_up": w2_up[0].astype(BF16), "w2_down": w2_down[0].astype(BF16),
        "w_proj": jnp.concatenate([wz, w_in[0][:, FOURIER_WIDTH:]], axis=1).astype(BF16),
        "w_out": w_out[0].astype(BF16),
        "attn_sink": attn_sink[0],
    }
    y_prompt = _trunk(x_prompt, w, n1=128, n2=128)
    y_sample = _trunk(x_sample, w, n1=64, n2=64)
    return (y_prompt, y_sample)
```

```python
import functools

import numpy as np
import jax
import jax.numpy as jnp
from jax import lax
from jax.experimental import pallas as pl
from jax.experimental.pallas import tpu as pltpu

D_MODEL = 1024
HEAD_DIM = 64
N_FOURIER_GROUPS = 8
FOURIER_WIDTH = N_FOURIER_GROUPS * HEAD_DIM
N_Q_HEADS = 8
N_KV_HEADS = 2
Q_PER_KV = N_Q_HEADS // N_KV_HEADS
ATTN_WIDTH = N_Q_HEADS * HEAD_DIM
KV_WIDTH = N_KV_HEADS * HEAD_DIM
D_FF = 2816
WINDOW = 128
BLOCK = 128
ROPE_THETA = 10000.0
RMS_EPS = 1e-6
NEG_INF = -1e30

SUBLANES = 8
LANES = 128
PROJ_WIDTH = 2 * FOURIER_WIDTH + ATTN_WIDTH + 2 * KV_WIDTH
VMEM_LIMIT_BYTES = 56 * 1024 * 1024
TOKEN_TILE = 512

F32 = jnp.float32
BF16 = jnp.bfloat16


def _rms(x, g):
    return x * lax.rsqrt(jnp.mean(x * x, axis=-1, keepdims=True) + RMS_EPS) * g


def _const_spec(shape):
    zeros = (0,) * len(shape)
    return pl.BlockSpec(shape, lambda *_: zeros, pipeline_mode=pl.Buffered(1))


def _params(n_axes):
    return pltpu.CompilerParams(dimension_semantics=("arbitrary",) * n_axes,
                                vmem_limit_bytes=VMEM_LIMIT_BYTES)


def _fold_kernel(wu_ref, wf_ref, cc_ref, sc_ref, o_ref):
    hi = lax.Precision.HIGHEST
    for g in range(N_FOURIER_GROUPS):
        wf = wf_ref[g]
        pr = jnp.dot(cc_ref[...], wf, precision=hi, preferred_element_type=F32)
        pi = jnp.dot(sc_ref[...], wf, precision=hi, preferred_element_type=F32)
        wug = wu_ref[:, g * HEAD_DIM:(g + 1) * HEAD_DIM]
        o_ref[:, g * HEAD_DIM:(g + 1) * HEAD_DIM] = jnp.dot(wug, pr, precision=hi, preferred_element_type=F32)
        o_ref[:, FOURIER_WIDTH + g * HEAD_DIM:FOURIER_WIDTH + (g + 1) * HEAD_DIM] = -jnp.dot(
            wug, pi, precision=hi, preferred_element_type=F32)


def _fold_weights(w_in_u, w_fourier):
    c = np.arange(HEAD_DIM)
    ang = 2.0 * np.pi * np.outer(c, c) / HEAD_DIM
    scale = HEAD_DIM ** -0.5
    cc = jnp.asarray(np.cos(ang) * scale, F32)
    sc = jnp.asarray(np.sin(ang) * scale, F32)
    return pl.pallas_call(
        _fold_kernel,
        out_shape=jax.ShapeDtypeStruct((D_MODEL, 2 * FOURIER_WIDTH), F32),
        name="fold",
    )(w_in_u, w_fourier, cc, sc)


def _ffn_kernel(x_ref, g_ref, wg_ref, wu_ref, wd_ref, o_ref):
    x = x_ref[...]
    xn = _rms(x, g_ref[...]).astype(BF16)
    gate = jnp.dot(xn, wg_ref[...], preferred_element_type=F32)
    up = jnp.dot(xn, wu_ref[...], preferred_element_type=F32)
    act = (gate * jax.nn.sigmoid(gate) * up).astype(BF16)
    o_ref[...] = x + 0.5 * jnp.dot(act, wd_ref[...], preferred_element_type=F32)


def _ffn(x, g, wg, wu, wd):
    t = x.shape[0]
    tm = TOKEN_TILE
    return pl.pallas_call(
        _ffn_kernel,
        grid=(t // tm,),
        in_specs=[
            pl.BlockSpec((tm, D_MODEL), lambda i: (i, 0)),
            _const_spec((1, D_MODEL)),
            _const_spec((D_MODEL, D_FF)),
            _const_spec((D_MODEL, D_FF)),
            _const_spec((D_FF, D_MODEL)),
        ],
        out_specs=pl.BlockSpec((tm, D_MODEL), lambda i: (i, 0)),
        out_shape=jax.ShapeDtypeStruct((t, D_MODEL), F32),
        compiler_params=_params(1),
        name="ffn",
    )(x, g, wg, wu, wd)


def _inproj_kernel(x_ref, g_ref, w_ref, cos_ref, sin_ref, twc_ref, tws_ref, d1_ref,
                   t_ref, q_ref, k_ref, v_ref, z_scr, *, n1):
    tm = n1 * SUBLANES
    x = x_ref[...].reshape(tm, D_MODEL)
    hb = _rms(x, g_ref[...]).astype(BF16)
    proj = jnp.dot(hb, w_ref[...], preferred_element_type=F32)
    for c in range(2 * FOURIER_WIDTH // LANES):
        z_scr[c] = proj[:, c * LANES:(c + 1) * LANES]

    cos = cos_ref[...].reshape(tm, LANES)
    sin = sin_ref[...].reshape(tm, LANES)
    lane = lax.broadcasted_iota(jnp.int32, (tm, LANES), 1)
    first_half = (lane % HEAD_DIM) < (HEAD_DIM // 2)

    def rope(xc):
        rot = jnp.where(first_half, pltpu.roll(xc, LANES - HEAD_DIM // 2, 1), pltpu.roll(xc, HEAD_DIM // 2, 1))
        return xc * cos + rot * sin

    q0 = 2 * FOURIER_WIDTH
    for c in range(ATTN_WIDTH // LANES):
        qc = rope(proj[:, q0 + c * LANES:q0 + (c + 1) * LANES]) * (HEAD_DIM ** -0.5)
        q_ref[:, :, c * LANES:(c + 1) * LANES] = qc.reshape(n1, SUBLANES, LANES)
    k0 = q0 + ATTN_WIDTH
    k_ref[...] = rope(proj[:, k0:k0 + KV_WIDTH]).reshape(n1, SUBLANES, KV_WIDTH)
    v_ref[...] = proj[:, k0 + KV_WIDTH:k0 + 2 * KV_WIDTH].reshape(n1, SUBLANES, KV_WIDTH)

    d1 = d1_ref[...]
    for j in range(SUBLANES):
        rows = pl.ds(j, n1, stride=SUBLANES)
        nch = FOURIER_WIDTH // LANES
        zr = jnp.concatenate([z_scr[c, rows, :] for c in range(nch)], axis=1)
        zi = jnp.concatenate([z_scr[nch + c, rows, :] for c in range(nch)], axis=1)
        st = jnp.concatenate([zr, zi], axis=0).astype(BF16)
        t = jnp.dot(d1, st, preferred_element_type=F32)
        tr, ti = t[:n1], t[n1:]
        c4 = jnp.concatenate([twc_ref[j]] * (FOURIER_WIDTH // LANES), axis=1)
        s4 = jnp.concatenate([tws_ref[j]] * (FOURIER_WIDTH // LANES), axis=1)
        t_ref[j, :n1, :] = tr * c4 + ti * s4
        t_ref[j, n1:, :] = ti * c4 - tr * s4


def _inproj(x1, g, w, cos_t, sin_t, twc, tws, d1, *, n1, n2):
    b = x1.shape[0]
    blk = lambda w_: pl.BlockSpec((None, n1, SUBLANES, w_), lambda bi, i: (bi, 0, i, 0))
    tab = pl.BlockSpec((n1, SUBLANES, LANES), lambda bi, i: (0, i, 0))
    tw = pl.BlockSpec((SUBLANES, n1, LANES), lambda bi, i: (i, 0, 0))
    return pl.pallas_call(
        functools.partial(_inproj_kernel, n1=n1),
        grid=(b, n2 // SUBLANES),
        in_specs=[blk(D_MODEL), _const_spec((1, D_MODEL)), _const_spec((D_MODEL, PROJ_WIDTH)),
                  tab, tab, tw, tw, _const_spec((2 * n1, 2 * n1))],
        out_specs=[
            pl.BlockSpec((None, SUBLANES, 2 * n1, FOURIER_WIDTH), lambda bi, i: (bi, i, 0, 0)),
            blk(ATTN_WIDTH), blk(KV_WIDTH), blk(KV_WIDTH),
        ],
        out_shape=[
            jax.ShapeDtypeStruct((b, n2, 2 * n1, FOURIER_WIDTH), F32),
            jax.ShapeDtypeStruct((b, n1, n2, ATTN_WIDTH), F32),
            jax.ShapeDtypeStruct((b, n1, n2, KV_WIDTH), F32),
            jax.ShapeDtypeStruct((b, n1, n2, KV_WIDTH), F32),
        ],
        scratch_shapes=[pltpu.VMEM((2 * FOURIER_WIDTH // LANES, n1 * SUBLANES, LANES), F32)],
        compiler_params=_params(2),
        name="inproj",
    )(x1, g, w, cos_t, sin_t, twc, tws, d1)


F32_MAX = float(np.finfo(np.float32).max)
GROUP_WIDTH = Q_PER_KV * HEAD_DIM


def _attn_kernel(sink_ref, q_ref, kp_ref, kc_ref, kn_ref, vp_ref, vc_ref, vn_ref, o_ref, *, nb, nblk_seq):
    i = pl.program_id(0)
    rows = (nb + 2) * BLOCK
    kfull = jnp.concatenate([kp_ref[...], kc_ref[...], kn_ref[...]], axis=0)
    vfull = jnp.concatenate([vp_ref[...], vc_ref[...], vn_ref[...]], axis=0)
    lo = lax.broadcasted_iota(jnp.int32, (rows, LANES), 1) < HEAD_DIM

    def dup(x):
        r = pltpu.roll(x, HEAD_DIM, 1)
        return jnp.where(lo, x, r), jnp.where(lo, r, x)

    kt = jnp.transpose(kfull).astype(BF16)
    zero_kt = jnp.zeros((HEAD_DIM, 3 * BLOCK), BF16)
    vv = dup(vfull)
    wkeys = 3 * BLOCK
    grp = lax.broadcasted_iota(jnp.int32, (wkeys, GROUP_WIDTH), 1) // HEAD_DIM
    ogrp = lax.broadcasted_iota(jnp.int32, (BLOCK, GROUP_WIDTH), 1) // HEAD_DIM
    kidx = lax.broadcasted_iota(jnp.int32, (BLOCK, BLOCK), 1)
    qidx = lax.broadcasted_iota(jnp.int32, (BLOCK, BLOCK), 0)
    cap_first = jnp.where(kidx >= qidx, F32_MAX, NEG_INF)
    cap_last = jnp.where(kidx <= qidx, F32_MAX, NEG_INF)

    def blockdiag(x):
        wide = jnp.concatenate([x, x], axis=1)
        return jnp.concatenate([jnp.where(grp == g, wide, 0.0) for g in range(Q_PER_KV)], axis=0).astype(BF16)

    def blockdiag_t(xt):
        return jnp.concatenate(
            [jnp.concatenate([xt if gg == g else zero_kt for gg in range(Q_PER_KV)], axis=1)
             for g in range(Q_PER_KV)], axis=0)

    for n in range(nb):
        pb = (i * nb + n) % nblk_seq
        capf = jnp.minimum(cap_first, jnp.where(pb > 0, F32_MAX, NEG_INF))
        capl = jnp.minimum(cap_last, jnp.where(pb < nblk_seq - 1, F32_MAX, NEG_INF))
        for kh in range(N_KV_HEADS):
            kbd_t = blockdiag_t(kt[kh * HEAD_DIM:(kh + 1) * HEAD_DIM, n * BLOCK:n * BLOCK + wkeys])
            vbd = blockdiag(vv[kh][n * BLOCK:n * BLOCK + wkeys])
            c0 = kh * GROUP_WIDTH
            qh = q_ref[n * BLOCK:(n + 1) * BLOCK, c0:c0 + GROUP_WIDTH].astype(BF16)
            s_all = jnp.dot(qh, kbd_t, preferred_element_type=F32)
            ps = []
            scale = None
            for g in range(Q_PER_KV - 1, -1, -1):
                s0 = jnp.minimum(s_all[:, g * wkeys:g * wkeys + BLOCK], capf)
                s1 = s_all[:, g * wkeys + BLOCK:g * wkeys + 2 * BLOCK]
                s2 = jnp.minimum(s_all[:, g * wkeys + 2 * BLOCK:(g + 1) * wkeys], capl)
                sink = sink_ref[kh * Q_PER_KV + g]
                m = jnp.max(jnp.maximum(jnp.maximum(s0, s1), s2), axis=-1, keepdims=True)
                m = jnp.maximum(m, sink)
                p = [jnp.exp((t - m).astype(BF16)) for t in (s0, s1, s2)]
                psum = (p[0] + p[1] + p[2]).astype(F32)
                den = jnp.sum(psum, axis=-1, keepdims=True) + jnp.exp(sink - m)
                rinv = 1.0 / den
                scale = rinv if scale is None else jnp.where(ogrp == g, rinv, scale)
                ps = p + ps
            p_all = jnp.concatenate(ps, axis=1)
            o_ref[n * BLOCK:(n + 1) * BLOCK, c0:c0 + GROUP_WIDTH] = jnp.dot(
                p_all, vbd, preferred_element_type=F32) * scale


def _attention(q, k, v, sink, *, seq, nb=8):
    t = q.shape[0]
    nblk = t // BLOCK
    tq = nb * BLOCK
    cur = lambda w_: pl.BlockSpec((tq, w_), lambda i: (i, 0))
    prev = pl.BlockSpec((BLOCK, KV_WIDTH), lambda i: (jnp.maximum(i * nb - 1, 0), 0))
    nxt = pl.BlockSpec((BLOCK, KV_WIDTH), lambda i: (jnp.minimum(i * nb + nb, nblk - 1), 0))
    return pl.pallas_call(
        functools.partial(_attn_kernel, nb=nb, nblk_seq=seq // BLOCK),
        grid=(t // tq,),
        in_specs=[pl.BlockSpec(memory_space=pltpu.SMEM), cur(ATTN_WIDTH),
                  prev, cur(KV_WIDTH), nxt, prev, cur(KV_WIDTH), nxt],
        out_specs=cur(ATTN_WIDTH),
        out_shape=jax.ShapeDtypeStruct((t, ATTN_WIDTH), F32),
        compiler_params=_params(1),
        name="attention",
    )(sink, q, k, k, k, v, v, v)


FF_CHUNKS = ((0, 1024), (1024, 2048), (2048, D_FF))


def _mixffn_kernel(tr_ref, ti_ref, x_ref, ya_ref, d2_ref, wo_ref, g2_ref, wg_ref, wu_ref, wd_ref, gf_ref,
                   o_ref, tr_scr, ti_scr, yf_scr, *, n2, n2h):
    tm = n2h * SUBLANES
    nch = FOURIER_WIDTH // LANES

    @pl.when(pl.program_id(2) == 0)
    def _():
        for c in range(nch):
            tr_scr[c] = tr_ref[:, :, c * LANES:(c + 1) * LANES].reshape(n2 * SUBLANES, LANES)
            ti_scr[c] = ti_ref[:, :, c * LANES:(c + 1) * LANES].reshape(n2 * SUBLANES, LANES)

    d2 = d2_ref[...]
    for j in range(SUBLANES):
        rows = pl.ds(j, n2, stride=SUBLANES)
        tr = jnp.concatenate([tr_scr[c, rows, :] for c in range(nch)], axis=1)
        ti = jnp.concatenate([ti_scr[c, rows, :] for c in range(nch)], axis=1)
        st = jnp.concatenate([tr, ti], axis=0).astype(BF16)
        yfj = jnp.dot(d2, st, preferred_element_type=F32)
        for c in range(nch):
            yf_scr[c, pl.ds(j, n2h, stride=SUBLANES), :] = yfj[:, c * LANES:(c + 1) * LANES]
    yf = jnp.concatenate([yf_scr[c] for c in range(nch)], axis=1).astype(BF16)
    ya = ya_ref[...].reshape(tm, ATTN_WIDTH).astype(BF16)
    x2 = (x_ref[...].reshape(tm, D_MODEL)
          + jnp.dot(yf, wo_ref[:FOURIER_WIDTH, :], preferred_element_type=F32)
          + jnp.dot(ya, wo_ref[FOURIER_WIDTH:, :], preferred_element_type=F32))

    xn = _rms(x2, g2_ref[...]).astype(BF16)
    acc = None
    for lo, hi in FF_CHUNKS:
        gate = jnp.dot(xn, wg_ref[:, lo:hi], preferred_element_type=F32)
        up = jnp.dot(xn, wu_ref[:, lo:hi], preferred_element_type=F32)
        act = (gate * jax.nn.sigmoid(gate) * up).astype(BF16)
        part = jnp.dot(act, wd_ref[lo:hi, :], preferred_element_type=F32)
        acc = part if acc is None else acc + part
    y = _rms(x2 + 0.5 * acc, gf_ref[...])
    o_ref[...] = y.reshape(n2h, SUBLANES, D_MODEL)


def _mixffn(t_arr, x1, ya, d2, wo, g2, wg, wu, wd, gf, *, n1, n2):
    b = x1.shape[0]
    halves = (n2 * SUBLANES) // TOKEN_TILE
    n2h = n2 // halves
    t_re = pl.BlockSpec((None, n2, SUBLANES, FOURIER_WIDTH), lambda bi, i, h: (bi, 0, i, 0))
    t_im = pl.BlockSpec((None, n2, SUBLANES, FOURIER_WIDTH), lambda bi, i, h: (bi, 0, n1 // SUBLANES + i, 0))
    blk = lambda w_: pl.BlockSpec((None, n2h, SUBLANES, w_), lambda bi, i, h: (bi, h, i, 0))
    return pl.pallas_call(
        functools.partial(_mixffn_kernel, n2=n2, n2h=n2h),
        grid=(b, n1 // SUBLANES, halves),
        in_specs=[t_re, t_im, blk(D_MODEL), blk(ATTN_WIDTH),
                  pl.BlockSpec((n2h, 2 * n2), lambda bi, i, h: (h, 0)),
                  _const_spec((2 * FOURIER_WIDTH, D_MODEL)), _const_spec((1, D_MODEL)),
                  _const_spec((D_MODEL, D_FF)), _const_spec((D_MODEL, D_FF)), _const_spec((D_FF, D_MODEL)),
                  _const_spec((1, D_MODEL))],
        out_specs=blk(D_MODEL),
        out_shape=jax.ShapeDtypeStruct((b, n2, n1, D_MODEL), F32),
        scratch_shapes=[pltpu.VMEM((FOURIER_WIDTH // LANES, n2 * SUBLANES, LANES), F32)] * 2
        + [pltpu.VMEM((FOURIER_WIDTH // LANES, n2h * SUBLANES, LANES), F32)],
        compiler_params=_params(3),
        name="mixffn",
    )(t_arr, t_arr, x1, ya, d2, wo, g2, wg, wu, wd, gf)


def _dft_tables(n1, n2):
    s = n1 * n2
    a1 = 2.0 * np.pi * np.outer(np.arange(n1), np.arange(n1)) / n1
    c1, s1 = np.cos(a1) / np.sqrt(n1), np.sin(a1) / np.sqrt(n1)
    d1 = np.block([[c1, s1], [-s1, c1]])
    a2 = 2.0 * np.pi * np.outer(np.arange(n2), np.arange(n2)) / n2
    d2 = np.concatenate([np.cos(a2), np.sin(a2)], axis=1) / np.sqrt(n2)
    m = (jnp.arange(n2, dtype=jnp.int32)[:, None] * jnp.arange(n1, dtype=jnp.int32)[None, :]) % s
    ang = m.astype(F32) * (2.0 * np.pi / s)
    twc = jnp.broadcast_to(jnp.cos(ang)[:, :, None], (n2, n1, LANES))
    tws = jnp.broadcast_to(jnp.sin(ang)[:, :, None], (n2, n1, LANES))
    return jnp.asarray(d1, F32).astype(BF16), jnp.asarray(d2, F32).astype(BF16), twc, tws


def _rope_tables(seq, n1, n2):
    half = HEAD_DIM // 2
    inv_freq = ROPE_THETA ** (-jnp.arange(half, dtype=F32) / half)
    ang = jnp.arange(seq).astype(F32)[:, None] * inv_freq[None, :]
    cos, sin = jnp.cos(ang), jnp.sin(ang)
    reps = LANES // HEAD_DIM
    cos_t = jnp.tile(jnp.concatenate([cos, cos], axis=1), (1, reps))
    sin_t = jnp.tile(jnp.concatenate([-sin, sin], axis=1), (1, reps))
    return cos_t.reshape(n1, n2, LANES), sin_t.reshape(n1, n2, LANES)


def _trunk(x, w, *, n1, n2):
    b, seq, _ = x.shape
    t = b * seq
    d1, d2, twc, tws = _dft_tables(n1, n2)
    cos_t, sin_t = _rope_tables(seq, n1, n2)
    x1 = _ffn(x.reshape(t, D_MODEL), w["g_ffn1"], w["w1_gate"], w["w1_up"], w["w1_down"])
    t_arr, q, k, v = _inproj(x1.reshape(b, n1, n2, D_MODEL), w["g_mix"], w["w_proj"], cos_t, sin_t,
                             twc, tws, d1, n1=n1, n2=n2)
    ya = _attention(q.reshape(t, ATTN_WIDTH), k.reshape(t, KV_WIDTH), v.reshape(t, KV_WIDTH), w["attn_sink"],
                    seq=seq)
    y = _mixffn(t_arr, x1.reshape(b, n2, n1, D_MODEL), ya.reshape(b, n2, n1, ATTN_WIDTH), d2, w["w_out"],
                w["g_ffn2"], w["w2_gate"], w["w2_up"], w["w2_down"], w["g_final"], n1=n1, n2=n2)
    return y.reshape(b, seq, D_MODEL)


def kernel(x_prompt, x_sample, g_ffn1, w1_gate, w1_up, w1_down, g_mix, w_in, w_fourier, attn_sink, w_out,
           g_ffn2, w2_gate, w2_up, w2_down, g_final):
    assert g_ffn1.shape[0] == 1, "single-layer trunk"
    wz = _fold_weights(w_in[0][:, :FOURIER_WIDTH], w_fourier[0])
    w = {
        "g_ffn1": g_ffn1[0][None, :], "g_mix": g_mix[0][None, :], "g_ffn2": g_ffn2[0][None, :],
        "g_final": g_final[None, :],
        "w1_gate": w1_gate[0].astype(BF16), "w1_up": w1_up[0].astype(BF16), "w1_down": w1_down[0].astype(BF16),
        "w2_gate": w2_gate[0].astype(BF16), "w2_up": w2_up[0].astype(BF16), "w2_down": w2_down[0].astype(BF16),
        "w_proj": jnp.concatenate([wz, w_in[0][:, FOURIER_WIDTH:]], axis=1).astype(BF16),
        "w_out": w_out[0].astype(BF16),
        "attn_sink": attn_sink[0],
    }
    y_prompt = _trunk(x_prompt, w, n1=128, n2=128)
    y_sample = _trunk(x_sample, w, n1=64, n2=64)
    return (y_prompt, y_sample)
```

```python
import functools

import numpy as np
import jax
import jax.numpy as jnp
from jax import lax
from jax.experimental import pallas as pl
from jax.experimental.pallas import tpu as pltpu

D_MODEL = 1024
HEAD_DIM = 64
N_FOURIER_GROUPS = 8
FOURIER_WIDTH = N_FOURIER_GROUPS * HEAD_DIM
N_Q_HEADS = 8
N_KV_HEADS = 2
Q_PER_KV = N_Q_HEADS // N_KV_HEADS
ATTN_WIDTH = N_Q_HEADS * HEAD_DIM
KV_WIDTH = N_KV_HEADS * HEAD_DIM
D_FF = 2816
WINDOW = 128
BLOCK = 128
ROPE_THETA = 10000.0
RMS_EPS = 1e-6
NEG_INF = -1e30

SUBLANES = 8
LANES = 128
PROJ_WIDTH = 2 * FOURIER_WIDTH + ATTN_WIDTH + 2 * KV_WIDTH
VMEM_LIMIT_BYTES = 56 * 1024 * 1024
TOKEN_TILE = 512

F32 = jnp.float32
BF16 = jnp.bfloat16


def _rms(x, g):
    return x * lax.rsqrt(jnp.mean(x * x, axis=-1, keepdims=True) + RMS_EPS) * g


def _const_spec(shape):
    zeros = (0,) * len(shape)
    return pl.BlockSpec(shape, lambda *_: zeros, pipeline_mode=pl.Buffered(1))


def _params(n_axes):
    return pltpu.CompilerParams(dimension_semantics=("arbitrary",) * n_axes,
                                vmem_limit_bytes=VMEM_LIMIT_BYTES)


def _fold_kernel(w_in_ref, wf_ref, cc_ref, sc_ref, o_ref):
    hi = lax.Precision.HIGHEST
    for g in range(N_FOURIER_GROUPS):
        wf = wf_ref[g]
        pr = jnp.dot(cc_ref[...], wf, precision=hi, preferred_element_type=F32)
        pi = jnp.dot(sc_ref[...], wf, precision=hi, preferred_element_type=F32)
        wug = w_in_ref[:, g * HEAD_DIM:(g + 1) * HEAD_DIM]
        o_ref[:, g * HEAD_DIM:(g + 1) * HEAD_DIM] = jnp.dot(
            wug, pr, precision=hi, preferred_element_type=F32).astype(BF16)
        o_ref[:, FOURIER_WIDTH + g * HEAD_DIM:FOURIER_WIDTH + (g + 1) * HEAD_DIM] = (-jnp.dot(
            wug, pi, precision=hi, preferred_element_type=F32)).astype(BF16)
    o_ref[:, 2 * FOURIER_WIDTH:] = w_in_ref[:, FOURIER_WIDTH:].astype(BF16)


def _fold_weights(w_in, w_fourier):
    c = np.arange(HEAD_DIM)
    ang = 2.0 * np.pi * np.outer(c, c) / HEAD_DIM
    scale = HEAD_DIM ** -0.5
    cc = jnp.asarray(np.cos(ang) * scale, F32)
    sc = jnp.asarray(np.sin(ang) * scale, F32)
    return pl.pallas_call(
        _fold_kernel,
        out_shape=jax.ShapeDtypeStruct((D_MODEL, PROJ_WIDTH), BF16),
        name="fold",
    )(w_in, w_fourier, cc, sc)


def _ffn_kernel(x_ref, g_ref, wg_ref, wu_ref, wd_ref, o_ref):
    half = x_ref.shape[0] // 2
    for h in range(2):
        rows = slice(h * half, (h + 1) * half)
        x = x_ref[rows, :]
        xn = _rms(x, g_ref[...]).astype(BF16)
        gate = jnp.dot(xn, wg_ref[...], preferred_element_type=F32)
        up = jnp.dot(xn, wu_ref[...], preferred_element_type=F32)
        act = (gate * jax.nn.sigmoid(gate) * up).astype(BF16)
        o_ref[rows, :] = x + 0.5 * jnp.dot(act, wd_ref[...], preferred_element_type=F32)


def _ffn(x, g, wg, wu, wd):
    t = x.shape[0]
    tm = TOKEN_TILE
    return pl.pallas_call(
        _ffn_kernel,
        grid=(t // tm,),
        in_specs=[
            pl.BlockSpec((tm, D_MODEL), lambda i: (i, 0)),
            _const_spec((1, D_MODEL)),
            _const_spec((D_MODEL, D_FF)),
            _const_spec((D_MODEL, D_FF)),
            _const_spec((D_FF, D_MODEL)),
        ],
        out_specs=pl.BlockSpec((tm, D_MODEL), lambda i: (i, 0)),
        out_shape=jax.ShapeDtypeStruct((t, D_MODEL), F32),
        compiler_params=_params(1),
        name="ffn",
    )(x, g, wg, wu, wd)


def _inproj_kernel(x_ref, g_ref, w_ref, cos_ref, sin_ref, twc_ref, tws_ref, d1_ref,
                   t_ref, q_ref, k_ref, v_ref, z_scr, *, n1):
    n1h = n1 // 2
    tmh = n1h * SUBLANES
    lane = lax.broadcasted_iota(jnp.int32, (tmh, LANES), 1)
    first_half = (lane % HEAD_DIM) < (HEAD_DIM // 2)
    q0 = 2 * FOURIER_WIDTH
    k0 = q0 + ATTN_WIDTH
    for h in range(2):
        s1s = slice(h * n1h, (h + 1) * n1h)
        x = x_ref[s1s].reshape(tmh, D_MODEL)
        hb = _rms(x, g_ref[...]).astype(BF16)
        proj = jnp.dot(hb, w_ref[...], preferred_element_type=F32)
        for c in range(2 * FOURIER_WIDTH // LANES):
            z_scr[c, h * tmh:(h + 1) * tmh, :] = proj[:, c * LANES:(c + 1) * LANES]

        cos = cos_ref[s1s].reshape(tmh, LANES)
        sin = sin_ref[s1s].reshape(tmh, LANES)

        def rope(xc):
            rot = jnp.where(first_half, pltpu.roll(xc, LANES - HEAD_DIM // 2, 1), pltpu.roll(xc, HEAD_DIM // 2, 1))
            return xc * cos + rot * sin

        for c in range(ATTN_WIDTH // LANES):
            qc = rope(proj[:, q0 + c * LANES:q0 + (c + 1) * LANES]) * (HEAD_DIM ** -0.5)
            q_ref[s1s, :, c * LANES:(c + 1) * LANES] = qc.reshape(n1h, SUBLANES, LANES)
        k_ref[s1s] = rope(proj[:, k0:k0 + KV_WIDTH]).reshape(n1h, SUBLANES, KV_WIDTH)
        v_ref[s1s] = proj[:, k0 + KV_WIDTH:k0 + 2 * KV_WIDTH].reshape(n1h, SUBLANES, KV_WIDTH)

    d1 = d1_ref[...]
    for j in range(SUBLANES):
        rows = pl.ds(j, n1, stride=SUBLANES)
        nch = FOURIER_WIDTH // LANES
        zr = jnp.concatenate([z_scr[c, rows, :] for c in range(nch)], axis=1)
        zi = jnp.concatenate([z_scr[nch + c, rows, :] for c in range(nch)], axis=1)
        st = jnp.concatenate([zr, zi], axis=0).astype(BF16)
        t = jnp.dot(d1, st, preferred_element_type=F32)
        tr, ti = t[:n1], t[n1:]
        c4 = jnp.concatenate([twc_ref[j]] * (FOURIER_WIDTH // LANES), axis=1)
        s4 = jnp.concatenate([tws_ref[j]] * (FOURIER_WIDTH // LANES), axis=1)
        t_ref[j, :n1, :] = tr * c4 + ti * s4
        t_ref[j, n1:, :] = ti * c4 - tr * s4


def _inproj(x1, g, w, cos_t, sin_t, twc, tws, d1, *, n1, n2):
    b = x1.shape[0]
    blk = lambda w_: pl.BlockSpec((None, n1, SUBLANES, w_), lambda bi, i: (bi, 0, i, 0))
    tab = pl.BlockSpec((n1, SUBLANES, LANES), lambda bi, i: (0, i, 0))
    tw = pl.BlockSpec((SUBLANES, n1, LANES), lambda bi, i: (i, 0, 0))
    return pl.pallas_call(
        functools.partial(_inproj_kernel, n1=n1),
        grid=(b, n2 // SUBLANES),
        in_specs=[blk(D_MODEL), _const_spec((1, D_MODEL)), _const_spec((D_MODEL, PROJ_WIDTH)),
                  tab, tab, tw, tw, _const_spec((2 * n1, 2 * n1))],
        out_specs=[
            pl.BlockSpec((None, SUBLANES, 2 * n1, FOURIER_WIDTH), lambda bi, i: (bi, i, 0, 0)),
            blk(ATTN_WIDTH), blk(KV_WIDTH), blk(KV_WIDTH),
        ],
        out_shape=[
            jax.ShapeDtypeStruct((b, n2, 2 * n1, FOURIER_WIDTH), F32),
            jax.ShapeDtypeStruct((b, n1, n2, ATTN_WIDTH), F32),
            jax.ShapeDtypeStruct((b, n1, n2, KV_WIDTH), F32),
            jax.ShapeDtypeStruct((b, n1, n2, KV_WIDTH), F32),
        ],
        scratch_shapes=[pltpu.VMEM((2 * FOURIER_WIDTH // LANES, n1 * SUBLANES, LANES), F32)],
        compiler_params=_params(2),
        name="inproj",
    )(x1, g, w, cos_t, sin_t, twc, tws, d1)


F32_MAX = float(np.finfo(np.float32).max)
GROUP_WIDTH = Q_PER_KV * HEAD_DIM


def _attn_kernel(sink_ref, q_ref, kp_ref, kc_ref, kn_ref, vp_ref, vc_ref, vn_ref, o_ref, *, nb, nblk_seq):
    i = pl.program_id(0)
    rows = (nb + 2) * BLOCK
    kfull = jnp.concatenate([kp_ref[...], kc_ref[...], kn_ref[...]], axis=0)
    vfull = jnp.concatenate([vp_ref[...], vc_ref[...], vn_ref[...]], axis=0)
    lo = lax.broadcasted_iota(jnp.int32, (rows, LANES), 1) < HEAD_DIM

    def dup(x):
        r = pltpu.roll(x, HEAD_DIM, 1)
        return jnp.where(lo, x, r), jnp.where(lo, r, x)

    kt = jnp.transpose(kfull).astype(BF16)
    zero_kt = jnp.zeros((HEAD_DIM, 3 * BLOCK), BF16)
    vv = dup(vfull)
    wkeys = 3 * BLOCK
    grp = lax.broadcasted_iota(jnp.int32, (wkeys, GROUP_WIDTH), 1) // HEAD_DIM
    ogrp = lax.broadcasted_iota(jnp.int32, (BLOCK, GROUP_WIDTH), 1) // HEAD_DIM
    kidx = lax.broadcasted_iota(jnp.int32, (BLOCK, BLOCK), 1)
    qidx = lax.broadcasted_iota(jnp.int32, (BLOCK, BLOCK), 0)
    cap_first = jnp.where(kidx >= qidx, F32_MAX, NEG_INF)
    cap_last = jnp.where(kidx <= qidx, F32_MAX, NEG_INF)

    def blockdiag(x):
        wide = jnp.concatenate([x, x], axis=1)
        return jnp.concatenate([jnp.where(grp == g, wide, 0.0) for g in range(Q_PER_KV)], axis=0).astype(BF16)

    def blockdiag_t(xt):
        return jnp.concatenate(
            [jnp.concatenate([xt if gg == g else zero_kt for gg in range(Q_PER_KV)], axis=1)
             for g in range(Q_PER_KV)], axis=0)

    for n in range(nb):
        pb = (i * nb + n) % nblk_seq
        capf = jnp.minimum(cap_first, jnp.where(pb > 0, F32_MAX, NEG_INF))
        capl = jnp.minimum(cap_last, jnp.where(pb < nblk_seq - 1, F32_MAX, NEG_INF))
        for kh in range(N_KV_HEADS):
            kbd_t = blockdiag_t(kt[kh * HEAD_DIM:(kh + 1) * HEAD_DIM, n * BLOCK:n * BLOCK + wkeys])
            vbd = blockdiag(vv[kh][n * BLOCK:n * BLOCK + wkeys])
            c0 = kh * GROUP_WIDTH
            qh = q_ref[n * BLOCK:(n + 1) * BLOCK, c0:c0 + GROUP_WIDTH].astype(BF16)
            s_all = jnp.dot(qh, kbd_t, preferred_element_type=F32)
            ps = []
            scale = None
            for g in range(Q_PER_KV - 1, -1, -1):
                s0 = jnp.minimum(s_all[:, g * wkeys:g * wkeys + BLOCK], capf)
                s1 = s_all[:, g * wkeys + BLOCK:g * wkeys + 2 * BLOCK]
                s2 = jnp.minimum(s_all[:, g * wkeys + 2 * BLOCK:(g + 1) * wkeys], capl)
                sink = sink_ref[kh * Q_PER_KV + g]
                m = jnp.max(jnp.maximum(jnp.maximum(s0, s1), s2), axis=-1, keepdims=True)
                m = jnp.maximum(m, sink)
                p = [jnp.exp((t - m).astype(BF16)) for t in (s0, s1, s2)]
                psum = (p[0] + p[1] + p[2]).astype(F32)
                den = jnp.sum(psum, axis=-1, keepdims=True) + jnp.exp(sink - m)
                rinv = 1.0 / den
                scale = rinv if scale is None else jnp.where(ogrp == g, rinv, scale)
                ps = p + ps
            p_all = jnp.concatenate(ps, axis=1)
            o_ref[n * BLOCK:(n + 1) * BLOCK, c0:c0 + GROUP_WIDTH] = jnp.dot(
                p_all, vbd, preferred_element_type=F32) * scale


def _attention(q, k, v, sink, *, seq, nb=8):
    t = q.shape[0]
    nblk = t // BLOCK
    tq = nb * BLOCK
    cur = lambda w_: pl.BlockSpec((tq, w_), lambda i: (i, 0))
    prev = pl.BlockSpec((BLOCK, KV_WIDTH), lambda i: (jnp.maximum(i * nb - 1, 0), 0))
    nxt = pl.BlockSpec((BLOCK, KV_WIDTH), lambda i: (jnp.minimum(i * nb + nb, nblk - 1), 0))
    return pl.pallas_call(
        functools.partial(_attn_kernel, nb=nb, nblk_seq=seq // BLOCK),
        grid=(t // tq,),
        in_specs=[pl.BlockSpec(memory_space=pltpu.SMEM), cur(ATTN_WIDTH),
                  prev, cur(KV_WIDTH), nxt, prev, cur(KV_WIDTH), nxt],
        out_specs=cur(ATTN_WIDTH),
        out_shape=jax.ShapeDtypeStruct((t, ATTN_WIDTH), F32),
        compiler_params=_params(1),
        name="attention",
    )(sink, q, k, k, k, v, v, v)


FF_CHUNKS = ((0, 1024), (1024, 2048), (2048, D_FF))


def _mixffn_kernel(tr_ref, ti_ref, x_ref, ya_ref, d2_ref, wo_ref, g2_ref, wg_ref, wu_ref, wd_ref, gf_ref,
                   o_ref, tr_scr, ti_scr, yf_scr, *, n2, n2h):
    tm = n2h * SUBLANES
    nch = FOURIER_WIDTH // LANES

    @pl.when(pl.program_id(2) == 0)
    def _():
        for c in range(nch):
            tr_scr[c] = tr_ref[:, :, c * LANES:(c + 1) * LANES].reshape(n2 * SUBLANES, LANES)
            ti_scr[c] = ti_ref[:, :, c * LANES:(c + 1) * LANES].reshape(n2 * SUBLANES, LANES)

    d2 = d2_ref[...]
    for j in range(SUBLANES):
        rows = pl.ds(j, n2, stride=SUBLANES)
        tr = jnp.concatenate([tr_scr[c, rows, :] for c in range(nch)], axis=1)
        ti = jnp.concatenate([ti_scr[c, rows, :] for c in range(nch)], axis=1)
        st = jnp.concatenate([tr, ti], axis=0).astype(BF16)
        yfj = jnp.dot(d2, st, preferred_element_type=F32)
        for c in range(nch):
            yf_scr[c, pl.ds(j, n2h, stride=SUBLANES), :] = yfj[:, c * LANES:(c + 1) * LANES]
    yf = jnp.concatenate([yf_scr[c] for c in range(nch)], axis=1).astype(BF16)
    ya = ya_ref[...].reshape(tm, ATTN_WIDTH).astype(BF16)
    x2 = (x_ref[...].reshape(tm, D_MODEL)
          + jnp.dot(yf, wo_ref[:FOURIER_WIDTH, :], preferred_element_type=F32)
          + jnp.dot(ya, wo_ref[FOURIER_WIDTH:, :], preferred_element_type=F32))

    half = tm // 2
    for h in range(2):
        xh = x2[h * half:(h + 1) * half]
        xn = _rms(xh, g2_ref[...]).astype(BF16)
        acc = None
        for lo, hi in FF_CHUNKS:
            gate = jnp.dot(xn, wg_ref[:, lo:hi], preferred_element_type=F32)
            up = jnp.dot(xn, wu_ref[:, lo:hi], preferred_element_type=F32)
            act = (gate * jax.nn.sigmoid(gate) * up).astype(BF16)
            part = jnp.dot(act, wd_ref[lo:hi, :], preferred_element_type=F32)
            acc = part if acc is None else acc + part
        y = _rms(xh + 0.5 * acc, gf_ref[...])
        o_ref[h * (n2h // 2):(h + 1) * (n2h // 2)] = y.reshape(n2h // 2, SUBLANES, D_MODEL)


def _mixffn(t_arr, x1, ya, d2, wo, g2, wg, wu, wd, gf, *, n1, n2):
    b = x1.shape[0]
    halves = (n2 * SUBLANES) // TOKEN_TILE
    n2h = n2 // halves
    t_re = pl.BlockSpec((None, n2, SUBLANES, FOURIER_WIDTH), lambda bi, i, h: (bi, 0, i, 0))
    t_im = pl.BlockSpec((None, n2, SUBLANES, FOURIER_WIDTH), lambda bi, i, h: (bi, 0, n1 // SUBLANES + i, 0))
    blk = lambda w_: pl.BlockSpec((None, n2h, SUBLANES, w_), lambda bi, i, h: (bi, h, i, 0))
    return pl.pallas_call(
        functools.partial(_mixffn_kernel, n2=n2, n2h=n2h),
        grid=(b, n1 // SUBLANES, halves),
        in_specs=[t_re, t_im, blk(D_MODEL), blk(ATTN_WIDTH),
                  pl.BlockSpec((n2h, 2 * n2), lambda bi, i, h: (h, 0)),
                  _const_spec((2 * FOURIER_WIDTH, D_MODEL)), _const_spec((1, D_MODEL)),
                  _const_spec((D_MODEL, D_FF)), _const_spec((D_MODEL, D_FF)), _const_spec((D_FF, D_MODEL)),
                  _const_spec((1, D_MODEL))],
        out_specs=blk(D_MODEL),
        out_shape=jax.ShapeDtypeStruct((b, n2, n1, D_MODEL), F32),
        scratch_shapes=[pltpu.VMEM((FOURIER_WIDTH // LANES, n2 * SUBLANES, LANES), F32)] * 2
        + [pltpu.VMEM((FOURIER_WIDTH // LANES, n2h * SUBLANES, LANES), F32)],
        compiler_params=_params(3),
        name="mixffn",
    )(t_arr, t_arr, x1, ya, d2, wo, g2, wg, wu, wd, gf)


def _dft_tables(n1, n2):
    s = n1 * n2
    a1 = 2.0 * np.pi * np.outer(np.arange(n1), np.arange(n1)) / n1
    c1, s1 = np.cos(a1) / np.sqrt(n1), np.sin(a1) / np.sqrt(n1)
    d1 = np.block([[c1, s1], [-s1, c1]])
    a2 = 2.0 * np.pi * np.outer(np.arange(n2), np.arange(n2)) / n2
    d2 = np.concatenate([np.cos(a2), np.sin(a2)], axis=1) / np.sqrt(n2)
    m = (jnp.arange(n2, dtype=jnp.int32)[:, None] * jnp.arange(n1, dtype=jnp.int32)[None, :]) % s
    ang = m.astype(F32) * (2.0 * np.pi / s)
    twc = jnp.broadcast_to(jnp.cos(ang)[:, :, None], (n2, n1, LANES))
    tws = jnp.broadcast_to(jnp.sin(ang)[:, :, None], (n2, n1, LANES))
    return jnp.asarray(d1, F32).astype(BF16), jnp.asarray(d2, F32).astype(BF16), twc, tws


def _rope_tables(seq, n1, n2):
    half = HEAD_DIM // 2
    lane = jnp.arange(LANES)
    inv_freq = ROPE_THETA ** (-(lane % half).astype(F32) / half)
    sign = jnp.where((lane % HEAD_DIM) < half, -1.0, 1.0).astype(F32)
    ang = jnp.arange(seq).astype(F32)[:, None] * inv_freq[None, :]
    return jnp.cos(ang).reshape(n1, n2, LANES), (jnp.sin(ang) * sign[None, :]).reshape(n1, n2, LANES)


def _trunk(x, w, *, n1, n2):
    b, seq, _ = x.shape
    t = b * seq
    d1, d2, twc, tws = _dft_tables(n1, n2)
    cos_t, sin_t = _rope_tables(seq, n1, n2)
    x1 = _ffn(x.reshape(t, D_MODEL), w["g_ffn1"], w["w1_gate"], w["w1_up"], w["w1_down"])
    t_arr, q, k, v = _inproj(x1.reshape(b, n1, n2, D_MODEL), w["g_mix"], w["w_proj"], cos_t, sin_t,
                             twc, tws, d1, n1=n1, n2=n2)
    ya = _attention(q.reshape(t, ATTN_WIDTH), k.reshape(t, KV_WIDTH), v.reshape(t, KV_WIDTH), w["attn_sink"],
                    seq=seq)
    y = _mixffn(t_arr, x1.reshape(b, n2, n1, D_MODEL), ya.reshape(b, n2, n1, ATTN_WIDTH), d2, w["w_out"],
                w["g_ffn2"], w["w2_gate"], w["w2_up"], w["w2_down"], w["g_final"], n1=n1, n2=n2)
    return y.reshape(b, seq, D_MODEL)


def kernel(x_prompt, x_sample, g_ffn1, w1_gate, w1_up, w1_down, g_mix, w_in, w_fourier, attn_sink, w_out,
           g_ffn2, w2_gate, w2_up, w2_down, g_final):
    assert g_ffn1.shape[0] == 1, "single-layer trunk"
    w = {
        "g_ffn1": g_ffn1[0][None, :], "g_mix": g_mix[0][None, :], "g_ffn2": g_ffn2[0][None, :],
        "g_final": g_final[None, :],
        "w1_gate": w1_gate[0].astype(BF16), "w1_up": w1_up[0].astype(BF16), "w1_down": w1_down[0].astype(BF16),
        "w2_gate": w2_gate[0].astype(BF16), "w2_up": w2_up[0].astype(BF16), "w2_down": w2_down[0].astype(BF16),
        "w_proj": _fold_weights(w_in[0], w_fourier[0]),
        "w_out": w_out[0].astype(BF16),
        "attn_sink": attn_sink[0],
    }
    y_prompt = _trunk(x_prompt, w, n1=128, n2=128)
    y_sample = _trunk(x_sample, w, n1=64, n2=64)
    return (y_prompt, y_sample)
```

```python
import functools

import numpy as np
import jax
import jax.numpy as jnp
from jax import lax
from jax.experimental import pallas as pl
from jax.experimental.pallas import tpu as pltpu

D_MODEL = 1024
HEAD_DIM = 64
N_FOURIER_GROUPS = 8
FOURIER_WIDTH = N_FOURIER_GROUPS * HEAD_DIM
N_Q_HEADS = 8
N_KV_HEADS = 2
Q_PER_KV = N_Q_HEADS // N_KV_HEADS
ATTN_WIDTH = N_Q_HEADS * HEAD_DIM
KV_WIDTH = N_KV_HEADS * HEAD_DIM
D_FF = 2816
WINDOW = 128
BLOCK = 128
ROPE_THETA = 10000.0
RMS_EPS = 1e-6
NEG_INF = -1e30

SUBLANES = 8
LANES = 128
PROJ_WIDTH = 2 * FOURIER_WIDTH + ATTN_WIDTH + 2 * KV_WIDTH
VMEM_LIMIT_BYTES = 56 * 1024 * 1024
TOKEN_TILE = 512

F32 = jnp.float32
BF16 = jnp.bfloat16


def _rms(x, g):
    return x * lax.rsqrt(jnp.mean(x * x, axis=-1, keepdims=True) + RMS_EPS) * g


def _const_spec(shape):
    zeros = (0,) * len(shape)
    return pl.BlockSpec(shape, lambda *_: zeros, pipeline_mode=pl.Buffered(1))


def _params(n_axes):
    return pltpu.CompilerParams(dimension_semantics=("arbitrary",) * n_axes,
                                vmem_limit_bytes=VMEM_LIMIT_BYTES)


def _fold_kernel(w_in_ref, wf_ref, cc_ref, sc_ref, o_ref):
    hi = lax.Precision.HIGHEST
    for g in range(N_FOURIER_GROUPS):
        wf = wf_ref[g]
        pr = jnp.dot(cc_ref[...], wf, precision=hi, preferred_element_type=F32)
        pi = jnp.dot(sc_ref[...], wf, precision=hi, preferred_element_type=F32)
        wug = w_in_ref[:, g * HEAD_DIM:(g + 1) * HEAD_DIM]
        o_ref[:, g * HEAD_DIM:(g + 1) * HEAD_DIM] = jnp.dot(
            wug, pr, precision=hi, preferred_element_type=F32).astype(BF16)
        o_ref[:, FOURIER_WIDTH + g * HEAD_DIM:FOURIER_WIDTH + (g + 1) * HEAD_DIM] = (-jnp.dot(
            wug, pi, precision=hi, preferred_element_type=F32)).astype(BF16)
    o_ref[:, 2 * FOURIER_WIDTH:] = w_in_ref[:, FOURIER_WIDTH:].astype(BF16)


def _fold_weights(w_in, w_fourier):
    c = np.arange(HEAD_DIM)
    ang = 2.0 * np.pi * np.outer(c, c) / HEAD_DIM
    scale = HEAD_DIM ** -0.5
    cc = jnp.asarray(np.cos(ang) * scale, F32)
    sc = jnp.asarray(np.sin(ang) * scale, F32)
    return pl.pallas_call(
        _fold_kernel,
        out_shape=jax.ShapeDtypeStruct((D_MODEL, PROJ_WIDTH), BF16),
        name="fold",
    )(w_in, w_fourier, cc, sc)


def _ffn_kernel(x_ref, g_ref, wg_ref, wu_ref, wd_ref, o_ref):
    half = x_ref.shape[0] // 2
    for h in range(2):
        rows = slice(h * half, (h + 1) * half)
        x = x_ref[rows, :]
        xn = _rms(x, g_ref[...]).astype(BF16)
        gate = jnp.dot(xn, wg_ref[...], preferred_element_type=F32)
        up = jnp.dot(xn, wu_ref[...], preferred_element_type=F32)
        act = (gate * jax.nn.sigmoid(gate) * up).astype(BF16)
        o_ref[rows, :] = x + 0.5 * jnp.dot(act, wd_ref[...], preferred_element_type=F32)


def _ffn(x, g, wg, wu, wd):
    t = x.shape[0]
    tm = TOKEN_TILE
    return pl.pallas_call(
        _ffn_kernel,
        grid=(t // tm,),
        in_specs=[
            pl.BlockSpec((tm, D_MODEL), lambda i: (i, 0)),
            _const_spec((1, D_MODEL)),
            _const_spec((D_MODEL, D_FF)),
            _const_spec((D_MODEL, D_FF)),
            _const_spec((D_FF, D_MODEL)),
        ],
        out_specs=pl.BlockSpec((tm, D_MODEL), lambda i: (i, 0)),
        out_shape=jax.ShapeDtypeStruct((t, D_MODEL), F32),
        compiler_params=_params(1),
        name="ffn",
    )(x, g, wg, wu, wd)


def _inproj_kernel(x_ref, g_ref, w_ref, ra_ref, rb_ref, ta_ref, tb_ref, d1_ref,
                   t_ref, q_ref, k_ref, v_ref, z_scr, *, n1):
    n1h = n1 // 2
    tmh = n1h * SUBLANES
    lane = lax.broadcasted_iota(jnp.int32, (tmh, LANES), 1)
    first_half = (lane % HEAD_DIM) < (HEAD_DIM // 2)
    q0 = 2 * FOURIER_WIDTH
    k0 = q0 + ATTN_WIDTH
    for h in range(2):
        s1s = slice(h * n1h, (h + 1) * n1h)
        x = x_ref[s1s].reshape(tmh, D_MODEL)
        hb = _rms(x, g_ref[...]).astype(BF16)
        proj = jnp.dot(hb, w_ref[...], preferred_element_type=F32)
        for c in range(2 * FOURIER_WIDTH // LANES):
            z_scr[c, h * tmh:(h + 1) * tmh, :] = proj[:, c * LANES:(c + 1) * LANES]

        def rows_a(t):
            return jnp.broadcast_to(t[:, None, :], (n1h, SUBLANES, LANES)).reshape(tmh, LANES)

        def rows_b(t):
            return jnp.broadcast_to(t[None, :, :], (n1h, SUBLANES, LANES)).reshape(tmh, LANES)

        ca, sa = rows_a(ra_ref[0, s1s, :]), rows_a(ra_ref[1, s1s, :])
        cos = ca * rows_b(rb_ref[0]) - sa * rows_b(rb_ref[1])
        sin = sa * rows_b(rb_ref[2]) + ca * rows_b(rb_ref[3])

        def rope(xc):
            rot = jnp.where(first_half, pltpu.roll(xc, LANES - HEAD_DIM // 2, 1), pltpu.roll(xc, HEAD_DIM // 2, 1))
            return xc * cos + rot * sin

        for c in range(ATTN_WIDTH // LANES):
            qc = rope(proj[:, q0 + c * LANES:q0 + (c + 1) * LANES]) * (HEAD_DIM ** -0.5)
            q_ref[s1s, :, c * LANES:(c + 1) * LANES] = qc.reshape(n1h, SUBLANES, LANES)
        k_ref[s1s] = rope(proj[:, k0:k0 + KV_WIDTH]).reshape(n1h, SUBLANES, KV_WIDTH)
        v_ref[s1s] = proj[:, k0 + KV_WIDTH:k0 + 2 * KV_WIDTH].reshape(n1h, SUBLANES, KV_WIDTH)

    d1 = d1_ref[...]
    for j in range(SUBLANES):
        rows = pl.ds(j, n1, stride=SUBLANES)
        nch = FOURIER_WIDTH // LANES
        zr = jnp.concatenate([z_scr[c, rows, :] for c in range(nch)], axis=1)
        zi = jnp.concatenate([z_scr[nch + c, rows, :] for c in range(nch)], axis=1)
        st = jnp.concatenate([zr, zi], axis=0).astype(BF16)
        t = jnp.dot(d1, st, preferred_element_type=F32)
        tr, ti = t[:n1], t[n1:]
        twc = ta_ref[0] * tb_ref[0, j] - ta_ref[1] * tb_ref[1, j]
        tws = ta_ref[1] * tb_ref[0, j] + ta_ref[0] * tb_ref[1, j]
        c4 = jnp.concatenate([twc] * (FOURIER_WIDTH // LANES), axis=1)
        s4 = jnp.concatenate([tws] * (FOURIER_WIDTH // LANES), axis=1)
        t_ref[j, :n1, :] = tr * c4 + ti * s4
        t_ref[j, n1:, :] = ti * c4 - tr * s4


def _inproj(x1, g, w, rope_a, rope_b, tw_a, tw_b, d1, *, n1, n2):
    b = x1.shape[0]
    blk = lambda w_: pl.BlockSpec((None, n1, SUBLANES, w_), lambda bi, i: (bi, 0, i, 0))
    tab_b = pl.BlockSpec((4, SUBLANES, LANES), lambda bi, i: (0, i, 0))
    tw_a_spec = pl.BlockSpec((2, None, n1, LANES), lambda bi, i: (0, i, 0, 0))
    return pl.pallas_call(
        functools.partial(_inproj_kernel, n1=n1),
        grid=(b, n2 // SUBLANES),
        in_specs=[blk(D_MODEL), _const_spec((1, D_MODEL)), _const_spec((D_MODEL, PROJ_WIDTH)),
                  _const_spec((2, n1, LANES)), tab_b, tw_a_spec, _const_spec((2, SUBLANES, n1, LANES)),
                  _const_spec((2 * n1, 2 * n1))],
        out_specs=[
            pl.BlockSpec((None, SUBLANES, 2 * n1, FOURIER_WIDTH), lambda bi, i: (bi, i, 0, 0)),
            blk(ATTN_WIDTH), blk(KV_WIDTH), blk(KV_WIDTH),
        ],
        out_shape=[
            jax.ShapeDtypeStruct((b, n2, 2 * n1, FOURIER_WIDTH), F32),
            jax.ShapeDtypeStruct((b, n1, n2, ATTN_WIDTH), F32),
            jax.ShapeDtypeStruct((b, n1, n2, KV_WIDTH), F32),
            jax.ShapeDtypeStruct((b, n1, n2, KV_WIDTH), F32),
        ],
        scratch_shapes=[pltpu.VMEM((2 * FOURIER_WIDTH // LANES, n1 * SUBLANES, LANES), F32)],
        compiler_params=_params(2),
        name="inproj",
    )(x1, g, w, rope_a, rope_b, tw_a, tw_b, d1)


F32_MAX = float(np.finfo(np.float32).max)
GROUP_WIDTH = Q_PER_KV * HEAD_DIM


def _attn_kernel(sink_ref, q_ref, kp_ref, kc_ref, kn_ref, vp_ref, vc_ref, vn_ref, o_ref, *, nb, nblk_seq):
    i = pl.program_id(0)
    rows = (nb + 2) * BLOCK
    kfull = jnp.concatenate([kp_ref[...], kc_ref[...], kn_ref[...]], axis=0)
    vfull = jnp.concatenate([vp_ref[...], vc_ref[...], vn_ref[...]], axis=0)
    lo = lax.broadcasted_iota(jnp.int32, (rows, LANES), 1) < HEAD_DIM

    def dup(x):
        r = pltpu.roll(x, HEAD_DIM, 1)
        return jnp.where(lo, x, r), jnp.where(lo, r, x)

    kt = jnp.transpose(kfull).astype(BF16)
    zero_kt = jnp.zeros((HEAD_DIM, 3 * BLOCK), BF16)
    vv = dup(vfull)
    wkeys = 3 * BLOCK
    grp = lax.broadcasted_iota(jnp.int32, (wkeys, GROUP_WIDTH), 1) // HEAD_DIM
    ogrp = lax.broadcasted_iota(jnp.int32, (BLOCK, GROUP_WIDTH), 1) // HEAD_DIM
    kidx = lax.broadcasted_iota(jnp.int32, (BLOCK, BLOCK), 1)
    qidx = lax.broadcasted_iota(jnp.int32, (BLOCK, BLOCK), 0)
    cap_first = jnp.where(kidx >= qidx, F32_MAX, NEG_INF)
    cap_last = jnp.where(kidx <= qidx, F32_MAX, NEG_INF)

    def blockdiag(x):
        wide = jnp.concatenate([x, x], axis=1)
        return jnp.concatenate([jnp.where(grp == g, wide, 0.0) for g in range(Q_PER_KV)], axis=0).astype(BF16)

    def blockdiag_t(xt):
        return jnp.concatenate(
            [jnp.concatenate([xt if gg == g else zero_kt for gg in range(Q_PER_KV)], axis=1)
             for g in range(Q_PER_KV)], axis=0)

    for n in range(nb):
        pb = (i * nb + n) % nblk_seq
        capf = jnp.minimum(cap_first, jnp.where(pb > 0, F32_MAX, NEG_INF))
        capl = jnp.minimum(cap_last, jnp.where(pb < nblk_seq - 1, F32_MAX, NEG_INF))
        for kh in range(N_KV_HEADS):
            kbd_t = blockdiag_t(kt[kh * HEAD_DIM:(kh + 1) * HEAD_DIM, n * BLOCK:n * BLOCK + wkeys])
            vbd = blockdiag(vv[kh][n * BLOCK:n * BLOCK + wkeys])
            c0 = kh * GROUP_WIDTH
            qh = q_ref[n * BLOCK:(n + 1) * BLOCK, c0:c0 + GROUP_WIDTH].astype(BF16)
            s_all = jnp.dot(qh, kbd_t, preferred_element_type=F32)
            ps = []
            scale = None
            for g in range(Q_PER_KV - 1, -1, -1):
                s0 = jnp.minimum(s_all[:, g * wkeys:g * wkeys + BLOCK], capf)
                s1 = s_all[:, g * wkeys + BLOCK:g * wkeys + 2 * BLOCK]
                s2 = jnp.minimum(s_all[:, g * wkeys + 2 * BLOCK:(g + 1) * wkeys], capl)
                sink = sink_ref[kh * Q_PER_KV + g]
                m = jnp.max(jnp.maximum(jnp.maximum(s0, s1), s2), axis=-1, keepdims=True)
                m = jnp.maximum(m, sink)
                p = [jnp.exp((t - m).astype(BF16)) for t in (s0, s1, s2)]
                psum = (p[0] + p[1] + p[2]).astype(F32)
                den = jnp.sum(psum, axis=-1, keepdims=True) + jnp.exp(sink - m)
                rinv = 1.0 / den
                scale = rinv if scale is None else jnp.where(ogrp == g, rinv, scale)
                ps = p + ps
            p_all = jnp.concatenate(ps, axis=1)
            o_ref[n * BLOCK:(n + 1) * BLOCK, c0:c0 + GROUP_WIDTH] = jnp.dot(
                p_all, vbd, preferred_element_type=F32) * scale


def _attention(q, k, v, sink, *, seq, nb=8):
    t = q.shape[0]
    nblk = t // BLOCK
    tq = nb * BLOCK
    cur = lambda w_: pl.BlockSpec((tq, w_), lambda i: (i, 0))
    prev = pl.BlockSpec((BLOCK, KV_WIDTH), lambda i: (jnp.maximum(i * nb - 1, 0), 0))
    nxt = pl.BlockSpec((BLOCK, KV_WIDTH), lambda i: (jnp.minimum(i * nb + nb, nblk - 1), 0))
    return pl.pallas_call(
        functools.partial(_attn_kernel, nb=nb, nblk_seq=seq // BLOCK),
        grid=(t // tq,),
        in_specs=[pl.BlockSpec(memory_space=pltpu.SMEM), cur(ATTN_WIDTH),
                  prev, cur(KV_WIDTH), nxt, prev, cur(KV_WIDTH), nxt],
        out_specs=cur(ATTN_WIDTH),
        out_shape=jax.ShapeDtypeStruct((t, ATTN_WIDTH), F32),
        compiler_params=_params(1),
        name="attention",
    )(sink, q, k, k, k, v, v, v)


FF_CHUNKS = ((0, 1024), (1024, 2048), (2048, D_FF))


def _mixffn_kernel(tr_ref, ti_ref, x_ref, ya_ref, d2_ref, wo_ref, g2_ref, wg_ref, wu_ref, wd_ref, gf_ref,
                   o_ref, tr_scr, ti_scr, yf_scr, *, n2, n2h):
    tm = n2h * SUBLANES
    nch = FOURIER_WIDTH // LANES

    @pl.when(pl.program_id(2) == 0)
    def _():
        for c in range(nch):
            tr_scr[c] = tr_ref[:, :, c * LANES:(c + 1) * LANES].reshape(n2 * SUBLANES, LANES)
            ti_scr[c] = ti_ref[:, :, c * LANES:(c + 1) * LANES].reshape(n2 * SUBLANES, LANES)

    d2 = d2_ref[...]
    for j in range(SUBLANES):
        rows = pl.ds(j, n2, stride=SUBLANES)
        tr = jnp.concatenate([tr_scr[c, rows, :] for c in range(nch)], axis=1)
        ti = jnp.concatenate([ti_scr[c, rows, :] for c in range(nch)], axis=1)
        st = jnp.concatenate([tr, ti], axis=0).astype(BF16)
        yfj = jnp.dot(d2, st, preferred_element_type=F32)
        for c in range(nch):
            yf_scr[c, pl.ds(j, n2h, stride=SUBLANES), :] = yfj[:, c * LANES:(c + 1) * LANES]
    yf = jnp.concatenate([yf_scr[c] for c in range(nch)], axis=1).astype(BF16)
    ya = ya_ref[...].reshape(tm, ATTN_WIDTH).astype(BF16)
    x2 = (x_ref[...].reshape(tm, D_MODEL)
          + jnp.dot(yf, wo_ref[:FOURIER_WIDTH, :], preferred_element_type=F32)
          + jnp.dot(ya, wo_ref[FOURIER_WIDTH:, :], preferred_element_type=F32))

    half = tm // 2
    for h in range(2):
        xh = x2[h * half:(h + 1) * half]
        xn = _rms(xh, g2_ref[...]).astype(BF16)
        acc = None
        for lo, hi in FF_CHUNKS:
            gate = jnp.dot(xn, wg_ref[:, lo:hi], preferred_element_type=F32)
            up = jnp.dot(xn, wu_ref[:, lo:hi], preferred_element_type=F32)
            act = (gate * jax.nn.sigmoid(gate) * up).astype(BF16)
            part = jnp.dot(act, wd_ref[lo:hi, :], preferred_element_type=F32)
            acc = part if acc is None else acc + part
        y = _rms(xh + 0.5 * acc, gf_ref[...])
        o_ref[h * (n2h // 2):(h + 1) * (n2h // 2)] = y.reshape(n2h // 2, SUBLANES, D_MODEL)


def _mixffn(t_arr, x1, ya, d2, wo, g2, wg, wu, wd, gf, *, n1, n2):
    b = x1.shape[0]
    halves = (n2 * SUBLANES) // TOKEN_TILE
    n2h = n2 // halves
    t_re = pl.BlockSpec((None, n2, SUBLANES, FOURIER_WIDTH), lambda bi, i, h: (bi, 0, i, 0))
    t_im = pl.BlockSpec((None, n2, SUBLANES, FOURIER_WIDTH), lambda bi, i, h: (bi, 0, n1 // SUBLANES + i, 0))
    blk = lambda w_: pl.BlockSpec((None, n2h, SUBLANES, w_), lambda bi, i, h: (bi, h, i, 0))
    return pl.pallas_call(
        functools.partial(_mixffn_kernel, n2=n2, n2h=n2h),
        grid=(b, n1 // SUBLANES, halves),
        in_specs=[t_re, t_im, blk(D_MODEL), blk(ATTN_WIDTH),
                  pl.BlockSpec((n2h, 2 * n2), lambda bi, i, h: (h, 0)),
                  _const_spec((2 * FOURIER_WIDTH, D_MODEL)), _const_spec((1, D_MODEL)),
                  _const_spec((D_MODEL, D_FF)), _const_spec((D_MODEL, D_FF)), _const_spec((D_FF, D_MODEL)),
                  _const_spec((1, D_MODEL))],
        out_specs=blk(D_MODEL),
        out_shape=jax.ShapeDtypeStruct((b, n2, n1, D_MODEL), F32),
        scratch_shapes=[pltpu.VMEM((FOURIER_WIDTH // LANES, n2 * SUBLANES, LANES), F32)] * 2
        + [pltpu.VMEM((FOURIER_WIDTH // LANES, n2h * SUBLANES, LANES), F32)],
        compiler_params=_params(3),
        name="mixffn",
    )(t_arr, t_arr, x1, ya, d2, wo, g2, wg, wu, wd, gf)


def _dft_tables(n1, n2):
    s = n1 * n2
    a1 = 2.0 * np.pi * np.outer(np.arange(n1), np.arange(n1)) / n1
    c1, s1 = np.cos(a1) / np.sqrt(n1), np.sin(a1) / np.sqrt(n1)
    d1 = np.block([[c1, s1], [-s1, c1]])
    a2 = 2.0 * np.pi * np.outer(np.arange(n2), np.arange(n2)) / n2
    d2 = np.concatenate([np.cos(a2), np.sin(a2)], axis=1) / np.sqrt(n2)
    k1 = jnp.arange(n1, dtype=jnp.int32)[None, :]

    def table(mult, count):
        m = (jnp.arange(count, dtype=jnp.int32)[:, None] * mult * k1) % s
        ang = m.astype(F32) * (2.0 * np.pi / s)
        return jnp.broadcast_to(jnp.stack([jnp.cos(ang), jnp.sin(ang)])[..., None], (2, count, n1, LANES))

    tw_a, tw_b = table(SUBLANES, n2 // SUBLANES), table(1, SUBLANES)
    return jnp.asarray(d1, F32).astype(BF16), jnp.asarray(d2, F32).astype(BF16), tw_a, tw_b


def _rope_tables(n1, n2):
    half = HEAD_DIM // 2
    lane = np.arange(LANES)
    inv_freq = ROPE_THETA ** (-(lane % half) / half)
    ang_a = (np.arange(n1) * n2)[:, None] * inv_freq[None, :]
    ang_b = np.arange(n2)[:, None] * inv_freq[None, :]
    sign = np.where((lane % HEAD_DIM) < half, -1.0, 1.0)[None, :]
    rope_a = np.stack([np.cos(ang_a), np.sin(ang_a)])
    rope_b = np.stack([np.cos(ang_b), np.sin(ang_b), sign * np.cos(ang_b), sign * np.sin(ang_b)])
    return jnp.asarray(rope_a, F32), jnp.asarray(rope_b, F32)


def _trunk(x, w, *, n1, n2):
    b, seq, _ = x.shape
    t = b * seq
    d1, d2, tw_a, tw_b = _dft_tables(n1, n2)
    rope_a, rope_b = _rope_tables(n1, n2)
    x1 = _ffn(x.reshape(t, D_MODEL), w["g_ffn1"], w["w1_gate"], w["w1_up"], w["w1_down"])
    t_arr, q, k, v = _inproj(x1.reshape(b, n1, n2, D_MODEL), w["g_mix"], w["w_proj"], rope_a, rope_b,
                             tw_a, tw_b, d1, n1=n1, n2=n2)
    ya = _attention(q.reshape(t, ATTN_WIDTH), k.reshape(t, KV_WIDTH), v.reshape(t, KV_WIDTH), w["attn_sink"],
                    seq=seq)
    y = _mixffn(t_arr, x1.reshape(b, n2, n1, D_MODEL), ya.reshape(b, n2, n1, ATTN_WIDTH), d2, w["w_out"],
                w["g_ffn2"], w["w2_gate"], w["w2_up"], w["w2_down"], w["g_final"], n1=n1, n2=n2)
    return y.reshape(b, seq, D_MODEL)


def kernel(x_prompt, x_sample, g_ffn1, w1_gate, w1_up, w1_down, g_mix, w_in, w_fourier, attn_sink, w_out,
           g_ffn2, w2_gate, w2_up, w2_down, g_final):
    assert g_ffn1.shape[0] == 1, "single-layer trunk"
    w = {
        "g_ffn1": g_ffn1[0][None, :], "g_mix": g_mix[0][None, :], "g_ffn2": g_ffn2[0][None, :],
        "g_final": g_final[None, :],
        "w1_gate": w1_gate[0].astype(BF16), "w1_up": w1_up[0].astype(BF16), "w1_down": w1_down[0].astype(BF16),
        "w2_gate": w2_gate[0].astype(BF16), "w2_up": w2_up[0].astype(BF16), "w2_down": w2_down[0].astype(BF16),
        "w_proj": _fold_weights(w_in[0], w_fourier[0]),
        "w_out": w_out[0].astype(BF16),
        "attn_sink": attn_sink[0],
    }
    y_prompt = _trunk(x_prompt, w, n1=128, n2=128)
    y_sample = _trunk(x_sample, w, n1=64, n2=64)
    return (y_prompt, y_sample)
```

```python
import functools

import numpy as np
import jax
import jax.numpy as jnp
from jax import lax
from jax.experimental import pallas as pl
from jax.experimental.pallas import tpu as pltpu

D_MODEL = 1024
HEAD_DIM = 64
N_FOURIER_GROUPS = 8
FOURIER_WIDTH = N_FOURIER_GROUPS * HEAD_DIM
N_Q_HEADS = 8
N_KV_HEADS = 2
Q_PER_KV = N_Q_HEADS // N_KV_HEADS
ATTN_WIDTH = N_Q_HEADS * HEAD_DIM
KV_WIDTH = N_KV_HEADS * HEAD_DIM
D_FF = 2816
WINDOW = 128
BLOCK = 128
ROPE_THETA = 10000.0
RMS_EPS = 1e-6
NEG_INF = -1e30

SUBLANES = 8
LANES = 128
PROJ_WIDTH = 2 * FOURIER_WIDTH + ATTN_WIDTH + 2 * KV_WIDTH
VMEM_LIMIT_BYTES = 56 * 1024 * 1024
TOKEN_TILE = 512
LOG2_E = 1.4426950408889634
Q_SCALE = HEAD_DIM ** -0.5 * LOG2_E

F32 = jnp.float32
BF16 = jnp.bfloat16


def _rms(x, g):
    return x * lax.rsqrt(jnp.mean(x * x, axis=-1, keepdims=True) + RMS_EPS) * g


def _const_spec(shape):
    zeros = (0,) * len(shape)
    return pl.BlockSpec(shape, lambda *_: zeros, pipeline_mode=pl.Buffered(1))


def _params(n_axes):
    return pltpu.CompilerParams(dimension_semantics=("arbitrary",) * n_axes,
                                vmem_limit_bytes=VMEM_LIMIT_BYTES)


def _fold_kernel(w_in_ref, wf_ref, cc_ref, sc_ref, o_ref):
    hi = lax.Precision.HIGHEST
    for g in range(N_FOURIER_GROUPS):
        wf = wf_ref[g]
        pr = jnp.dot(cc_ref[...], wf, precision=hi, preferred_element_type=F32)
        pi = jnp.dot(sc_ref[...], wf, precision=hi, preferred_element_type=F32)
        wug = w_in_ref[:, g * HEAD_DIM:(g + 1) * HEAD_DIM]
        o_ref[:, g * HEAD_DIM:(g + 1) * HEAD_DIM] = jnp.dot(
            wug, pr, precision=hi, preferred_element_type=F32).astype(BF16)
        o_ref[:, FOURIER_WIDTH + g * HEAD_DIM:FOURIER_WIDTH + (g + 1) * HEAD_DIM] = (-jnp.dot(
            wug, pi, precision=hi, preferred_element_type=F32)).astype(BF16)
    o_ref[:, 2 * FOURIER_WIDTH:] = w_in_ref[:, FOURIER_WIDTH:].astype(BF16)


def _fold_weights(w_in, w_fourier):
    c = np.arange(HEAD_DIM)
    ang = 2.0 * np.pi * np.outer(c, c) / HEAD_DIM
    scale = HEAD_DIM ** -0.5
    cc = jnp.asarray(np.cos(ang) * scale, F32)
    sc = jnp.asarray(np.sin(ang) * scale, F32)
    return pl.pallas_call(
        _fold_kernel,
        out_shape=jax.ShapeDtypeStruct((D_MODEL, PROJ_WIDTH), BF16),
        name="fold",
    )(w_in, w_fourier, cc, sc)


FF_CHUNKS = ((0, 1024), (1024, 2048), (2048, D_FF))


def _swiglu(xn, wg_ref, wu_ref, wd_ref):
    acc = None
    for lo, hi in FF_CHUNKS:
        gate = jnp.dot(xn, wg_ref[:, lo:hi], preferred_element_type=F32)
        up = jnp.dot(xn, wu_ref[:, lo:hi], preferred_element_type=F32)
        act = (gate * jax.nn.sigmoid(gate) * up).astype(BF16)
        part = jnp.dot(act, wd_ref[lo:hi, :], preferred_element_type=F32)
        acc = part if acc is None else acc + part
    return acc


def _ffn_kernel(x_ref, g_ref, wg_ref, wu_ref, wd_ref, o_ref):
    for h in range(x_ref.shape[0] // TOKEN_TILE):
        rows = slice(h * TOKEN_TILE, (h + 1) * TOKEN_TILE)
        x = x_ref[rows, :]
        xn = _rms(x, g_ref[...]).astype(BF16)
        o_ref[rows, :] = x + 0.5 * _swiglu(xn, wg_ref, wu_ref, wd_ref)


def _ffn(x, g, wg, wu, wd):
    t = x.shape[0]
    tm = 2 * TOKEN_TILE
    return pl.pallas_call(
        _ffn_kernel,
        grid=(t // tm,),
        in_specs=[
            pl.BlockSpec((tm, D_MODEL), lambda i: (i, 0)),
            _const_spec((1, D_MODEL)),
            _const_spec((D_MODEL, D_FF)),
            _const_spec((D_MODEL, D_FF)),
            _const_spec((D_FF, D_MODEL)),
        ],
        out_specs=pl.BlockSpec((tm, D_MODEL), lambda i: (i, 0)),
        out_shape=jax.ShapeDtypeStruct((t, D_MODEL), F32),
        compiler_params=_params(1),
        name="ffn",
    )(x, g, wg, wu, wd)


def _inproj_kernel(x_ref, g_ref, w_ref, ra_ref, rb_ref, ta_ref, tb_ref, d1_ref,
                   t_ref, q_ref, k_ref, v_ref, z_scr, *, n1):
    n1h = n1 // 2
    tmh = n1h * SUBLANES
    lane = lax.broadcasted_iota(jnp.int32, (tmh, LANES), 1)
    first_half = (lane % HEAD_DIM) < (HEAD_DIM // 2)
    q0 = 2 * FOURIER_WIDTH
    k0 = q0 + ATTN_WIDTH
    for h in range(2):
        s1s = slice(h * n1h, (h + 1) * n1h)
        x = x_ref[s1s].reshape(tmh, D_MODEL)
        hb = _rms(x, g_ref[...]).astype(BF16)
        proj = jnp.dot(hb, w_ref[...], preferred_element_type=F32)
        for c in range(2 * FOURIER_WIDTH // LANES):
            z_scr[c, h * tmh:(h + 1) * tmh, :] = proj[:, c * LANES:(c + 1) * LANES]

        def rows_a(t):
            return jnp.broadcast_to(t[:, None, :], (n1h, SUBLANES, LANES)).reshape(tmh, LANES)

        def rows_b(t):
            return jnp.broadcast_to(t[None, :, :], (n1h, SUBLANES, LANES)).reshape(tmh, LANES)

        ca, sa = rows_a(ra_ref[0, s1s, :]), rows_a(ra_ref[1, s1s, :])
        cos = ca * rows_b(rb_ref[0]) - sa * rows_b(rb_ref[1])
        sin = sa * rows_b(rb_ref[2]) + ca * rows_b(rb_ref[3])

        def rope(xc):
            rot = jnp.where(first_half, pltpu.roll(xc, LANES - HEAD_DIM // 2, 1), pltpu.roll(xc, HEAD_DIM // 2, 1))
            return xc * cos + rot * sin

        for c in range(ATTN_WIDTH // LANES):
            qc = rope(proj[:, q0 + c * LANES:q0 + (c + 1) * LANES]) * Q_SCALE
            q_ref[s1s, :, c * LANES:(c + 1) * LANES] = qc.reshape(n1h, SUBLANES, LANES)
        k_ref[s1s] = rope(proj[:, k0:k0 + KV_WIDTH]).reshape(n1h, SUBLANES, KV_WIDTH)
        v_ref[s1s] = proj[:, k0 + KV_WIDTH:k0 + 2 * KV_WIDTH].reshape(n1h, SUBLANES, KV_WIDTH)

    d1 = d1_ref[...]
    for j in range(SUBLANES):
        rows = pl.ds(j, n1, stride=SUBLANES)
        nch = FOURIER_WIDTH // LANES
        zr = jnp.concatenate([z_scr[c, rows, :] for c in range(nch)], axis=1)
        zi = jnp.concatenate([z_scr[nch + c, rows, :] for c in range(nch)], axis=1)
        st = jnp.concatenate([zr, zi], axis=0).astype(BF16)
        t = jnp.dot(d1, st, preferred_element_type=F32)
        tr, ti = t[:n1], t[n1:]
        twc = ta_ref[0] * tb_ref[0, j] - ta_ref[1] * tb_ref[1, j]
        tws = ta_ref[1] * tb_ref[0, j] + ta_ref[0] * tb_ref[1, j]
        c4 = jnp.concatenate([twc] * (FOURIER_WIDTH // LANES), axis=1)
        s4 = jnp.concatenate([tws] * (FOURIER_WIDTH // LANES), axis=1)
        t_ref[j, :n1, :] = tr * c4 + ti * s4
        t_ref[j, n1:, :] = ti * c4 - tr * s4


def _inproj(x1, g, w, rope_a, rope_b, tw_a, tw_b, d1, *, n1, n2):
    b = x1.shape[0]
    blk = lambda w_: pl.BlockSpec((None, n1, SUBLANES, w_), lambda bi, i: (bi, 0, i, 0))
    tab_b = pl.BlockSpec((4, SUBLANES, LANES), lambda bi, i: (0, i, 0))
    tw_a_spec = pl.BlockSpec((2, None, n1, LANES), lambda bi, i: (0, i, 0, 0))
    return pl.pallas_call(
        functools.partial(_inproj_kernel, n1=n1),
        grid=(b, n2 // SUBLANES),
        in_specs=[blk(D_MODEL), _const_spec((1, D_MODEL)), _const_spec((D_MODEL, PROJ_WIDTH)),
                  _const_spec((2, n1, LANES)), tab_b, tw_a_spec, _const_spec((2, SUBLANES, n1, LANES)),
                  _const_spec((2 * n1, 2 * n1))],
        out_specs=[
            pl.BlockSpec((None, SUBLANES, 2 * n1, FOURIER_WIDTH), lambda bi, i: (bi, i, 0, 0)),
            blk(ATTN_WIDTH), blk(KV_WIDTH), blk(KV_WIDTH),
        ],
        out_shape=[
            jax.ShapeDtypeStruct((b, n2, 2 * n1, FOURIER_WIDTH), F32),
            jax.ShapeDtypeStruct((b, n1, n2, ATTN_WIDTH), F32),
            jax.ShapeDtypeStruct((b, n1, n2, KV_WIDTH), F32),
            jax.ShapeDtypeStruct((b, n1, n2, KV_WIDTH), F32),
        ],
        scratch_shapes=[pltpu.VMEM((2 * FOURIER_WIDTH // LANES, n1 * SUBLANES, LANES), F32)],
        compiler_params=_params(2),
        name="inproj",
    )(x1, g, w, rope_a, rope_b, tw_a, tw_b, d1)


F32_MAX = float(np.finfo(np.float32).max)
GROUP_WIDTH = Q_PER_KV * HEAD_DIM


def _attn_kernel(sink_ref, q_ref, kp_ref, kc_ref, kn_ref, vp_ref, vc_ref, vn_ref, o_ref, *, nb, nblk_seq):
    i = pl.program_id(0)
    rows = (nb + 2) * BLOCK
    kfull = jnp.concatenate([kp_ref[...], kc_ref[...], kn_ref[...]], axis=0)
    vfull = jnp.concatenate([vp_ref[...], vc_ref[...], vn_ref[...]], axis=0)
    lo = lax.broadcasted_iota(jnp.int32, (rows, LANES), 1) < HEAD_DIM

    def dup(x):
        r = pltpu.roll(x, HEAD_DIM, 1)
        return jnp.where(lo, x, r), jnp.where(lo, r, x)

    kt = jnp.transpose(kfull).astype(BF16)
    zero_kt = jnp.zeros((HEAD_DIM, 3 * BLOCK), BF16)
    vv = dup(vfull)
    wkeys = 3 * BLOCK
    grp = lax.broadcasted_iota(jnp.int32, (wkeys, GROUP_WIDTH), 1) // HEAD_DIM
    ogrp = lax.broadcasted_iota(jnp.int32, (BLOCK, GROUP_WIDTH), 1) // HEAD_DIM
    kidx = lax.broadcasted_iota(jnp.int32, (BLOCK, BLOCK), 1)
    qidx = lax.broadcasted_iota(jnp.int32, (BLOCK, BLOCK), 0)
    cap_first = jnp.where(kidx >= qidx, F32_MAX, NEG_INF)
    cap_last = jnp.where(kidx <= qidx, F32_MAX, NEG_INF)

    def blockdiag(x):
        wide = jnp.concatenate([x, x], axis=1)
        return jnp.concatenate([jnp.where(grp == g, wide, 0.0) for g in range(Q_PER_KV)], axis=0).astype(BF16)

    def blockdiag_t(xt):
        return jnp.concatenate(
            [jnp.concatenate([xt if gg == g else zero_kt for gg in range(Q_PER_KV)], axis=1)
             for g in range(Q_PER_KV)], axis=0)

    for n in range(nb):
        pb = (i * nb + n) % nblk_seq
        capf = jnp.minimum(cap_first, jnp.where(pb > 0, F32_MAX, NEG_INF))
        capl = jnp.minimum(cap_last, jnp.where(pb < nblk_seq - 1, F32_MAX, NEG_INF))
        for kh in range(N_KV_HEADS):
            kbd_t = blockdiag_t(kt[kh * HEAD_DIM:(kh + 1) * HEAD_DIM, n * BLOCK:n * BLOCK + wkeys])
            vbd = blockdiag(vv[kh][n * BLOCK:n * BLOCK + wkeys])
            c0 = kh * GROUP_WIDTH
            qh = q_ref[n * BLOCK:(n + 1) * BLOCK, c0:c0 + GROUP_WIDTH].astype(BF16)
            s_all = jnp.dot(qh, kbd_t, preferred_element_type=F32)
            ps = []
            scale = None
            for g in range(Q_PER_KV - 1, -1, -1):
                s0 = jnp.minimum(s_all[:, g * wkeys:g * wkeys + BLOCK], capf)
                s1 = s_all[:, g * wkeys + BLOCK:g * wkeys + 2 * BLOCK]
                s2 = jnp.minimum(s_all[:, g * wkeys + 2 * BLOCK:(g + 1) * wkeys], capl)
                sink = sink_ref[kh * Q_PER_KV + g] * LOG2_E
                m = jnp.max(jnp.maximum(jnp.maximum(s0, s1), s2), axis=-1, keepdims=True)
                m = jnp.maximum(m, sink)
                p = [jnp.exp2((t - m).astype(BF16)) for t in (s0, s1, s2)]
                psum = (p[0] + p[1] + p[2]).astype(F32)
                den = jnp.sum(psum, axis=-1, keepdims=True) + jnp.exp2(sink - m)
                rinv = 1.0 / den
                scale = rinv if scale is None else jnp.where(ogrp == g, rinv, scale)
                ps = p + ps
            p_all = jnp.concatenate(ps, axis=1)
            o_ref[n * BLOCK:(n + 1) * BLOCK, c0:c0 + GROUP_WIDTH] = jnp.dot(
                p_all, vbd, preferred_element_type=F32) * scale


def _attention(q, k, v, sink, *, seq, nb=8):
    t = q.shape[0]
    nblk = t // BLOCK
    tq = nb * BLOCK
    cur = lambda w_: pl.BlockSpec((tq, w_), lambda i: (i, 0))
    prev = pl.BlockSpec((BLOCK, KV_WIDTH), lambda i: (jnp.maximum(i * nb - 1, 0), 0))
    nxt = pl.BlockSpec((BLOCK, KV_WIDTH), lambda i: (jnp.minimum(i * nb + nb, nblk - 1), 0))
    return pl.pallas_call(
        functools.partial(_attn_kernel, nb=nb, nblk_seq=seq // BLOCK),
        grid=(t // tq,),
        in_specs=[pl.BlockSpec(memory_space=pltpu.SMEM), cur(ATTN_WIDTH),
                  prev, cur(KV_WIDTH), nxt, prev, cur(KV_WIDTH), nxt],
        out_specs=cur(ATTN_WIDTH),
        out_shape=jax.ShapeDtypeStruct((t, ATTN_WIDTH), F32),
        compiler_params=_params(1),
        name="attention",
    )(sink, q, k, k, k, v, v, v)


def _mixffn_kernel(tr_ref, ti_ref, x_ref, ya_ref, d2_ref, wo_ref, g2_ref, wg_ref, wu_ref, wd_ref, gf_ref,
                   o_ref, tr_scr, ti_scr, yf_scr, *, n2, n2h):
    tm = n2h * SUBLANES
    nch = FOURIER_WIDTH // LANES

    @pl.when(pl.program_id(2) == 0)
    def _():
        for c in range(nch):
            tr_scr[c] = tr_ref[:, :, c * LANES:(c + 1) * LANES].reshape(n2 * SUBLANES, LANES)
            ti_scr[c] = ti_ref[:, :, c * LANES:(c + 1) * LANES].reshape(n2 * SUBLANES, LANES)

    d2 = d2_ref[...]
    for j in range(SUBLANES):
        rows = pl.ds(j, n2, stride=SUBLANES)
        tr = jnp.concatenate([tr_scr[c, rows, :] for c in range(nch)], axis=1)
        ti = jnp.concatenate([ti_scr[c, rows, :] for c in range(nch)], axis=1)
        st = jnp.concatenate([tr, ti], axis=0).astype(BF16)
        yfj = jnp.dot(d2, st, preferred_element_type=F32)
        for c in range(nch):
            yf_scr[c, pl.ds(j, n2h, stride=SUBLANES), :] = yfj[:, c * LANES:(c + 1) * LANES]
    yf = jnp.concatenate([yf_scr[c] for c in range(nch)], axis=1).astype(BF16)
    ya = ya_ref[...].reshape(tm, ATTN_WIDTH).astype(BF16)
    x2 = (x_ref[...].reshape(tm, D_MODEL)
          + jnp.dot(yf, wo_ref[:FOURIER_WIDTH, :], preferred_element_type=F32)
          + jnp.dot(ya, wo_ref[FOURIER_WIDTH:, :], preferred_element_type=F32))

    half = tm // 2
    for h in range(2):
        xh = x2[h * half:(h + 1) * half]
        xn = _rms(xh, g2_ref[...]).astype(BF16)
        y = _rms(xh + 0.5 * _swiglu(xn, wg_ref, wu_ref, wd_ref), gf_ref[...])
        o_ref[h * (n2h // 2):(h + 1) * (n2h // 2)] = y.reshape(n2h // 2, SUBLANES, D_MODEL)


def _mixffn(t_arr, x1, ya, d2, wo, g2, wg, wu, wd, gf, *, n1, n2):
    b = x1.shape[0]
    halves = (n2 * SUBLANES) // TOKEN_TILE
    n2h = n2 // halves
    t_re = pl.BlockSpec((None, n2, SUBLANES, FOURIER_WIDTH), lambda bi, i, h: (bi, 0, i, 0))
    t_im = pl.BlockSpec((None, n2, SUBLANES, FOURIER_WIDTH), lambda bi, i, h: (bi, 0, n1 // SUBLANES + i, 0))
    blk = lambda w_: pl.BlockSpec((None, n2h, SUBLANES, w_), lambda bi, i, h: (bi, h, i, 0))
    return pl.pallas_call(
        functools.partial(_mixffn_kernel, n2=n2, n2h=n2h),
        grid=(b, n1 // SUBLANES, halves),
        in_specs=[t_re, t_im, blk(D_MODEL), blk(ATTN_WIDTH),
                  pl.BlockSpec((n2h, 2 * n2), lambda bi, i, h: (h, 0)),
                  _const_spec((2 * FOURIER_WIDTH, D_MODEL)), _const_spec((1, D_MODEL)),
                  _const_spec((D_MODEL, D_FF)), _const_spec((D_MODEL, D_FF)), _const_spec((D_FF, D_MODEL)),
                  _const_spec((1, D_MODEL))],
        out_specs=blk(D_MODEL),
        out_shape=jax.ShapeDtypeStruct((b, n2, n1, D_MODEL), F32),
        scratch_shapes=[pltpu.VMEM((FOURIER_WIDTH // LANES, n2 * SUBLANES, LANES), F32)] * 2
        + [pltpu.VMEM((FOURIER_WIDTH // LANES, n2h * SUBLANES, LANES), F32)],
        compiler_params=_params(3),
        name="mixffn",
    )(t_arr, t_arr, x1, ya, d2, wo, g2, wg, wu, wd, gf)


def _dft_tables(n1, n2):
    s = n1 * n2
    a1 = 2.0 * np.pi * np.outer(np.arange(n1), np.arange(n1)) / n1
    c1, s1 = np.cos(a1) / np.sqrt(n1), np.sin(a1) / np.sqrt(n1)
    d1 = np.block([[c1, s1], [-s1, c1]])
    a2 = 2.0 * np.pi * np.outer(np.arange(n2), np.arange(n2)) / n2
    d2 = np.concatenate([np.cos(a2), np.sin(a2)], axis=1) / np.sqrt(n2)
    k1 = jnp.arange(n1, dtype=jnp.int32)[None, :]

    def table(mult, count):
        m = (jnp.arange(count, dtype=jnp.int32)[:, None] * mult * k1) % s
        ang = m.astype(F32) * (2.0 * np.pi / s)
        return jnp.broadcast_to(jnp.stack([jnp.cos(ang), jnp.sin(ang)])[..., None], (2, count, n1, LANES))

    tw_a, tw_b = table(SUBLANES, n2 // SUBLANES), table(1, SUBLANES)
    return jnp.asarray(d1, F32).astype(BF16), jnp.asarray(d2, F32).astype(BF16), tw_a, tw_b


def _rope_tables(n1, n2):
    half = HEAD_DIM // 2
    lane = np.arange(LANES)
    inv_freq = ROPE_THETA ** (-(lane % half) / half)
    ang_a = (np.arange(n1) * n2)[:, None] * inv_freq[None, :]
    ang_b = np.arange(n2)[:, None] * inv_freq[None, :]
    sign = np.where((lane % HEAD_DIM) < half, -1.0, 1.0)[None, :]
    rope_a = np.stack([np.cos(ang_a), np.sin(ang_a)])
    rope_b = np.stack([np.cos(ang_b), np.sin(ang_b), sign * np.cos(ang_b), sign * np.sin(ang_b)])
    return jnp.asarray(rope_a, F32), jnp.asarray(rope_b, F32)


def _trunk(x, w, *, n1, n2):
    b, seq, _ = x.shape
    t = b * seq
    d1, d2, tw_a, tw_b = _dft_tables(n1, n2)
    rope_a, rope_b = _rope_tables(n1, n2)
    x1 = _ffn(x.reshape(t, D_MODEL), w["g_ffn1"], w["w1_gate"], w["w1_up"], w["w1_down"])
    t_arr, q, k, v = _inproj(x1.reshape(b, n1, n2, D_MODEL), w["g_mix"], w["w_proj"], rope_a, rope_b,
                             tw_a, tw_b, d1, n1=n1, n2=n2)
    ya = _attention(q.reshape(t, ATTN_WIDTH), k.reshape(t, KV_WIDTH), v.reshape(t, KV_WIDTH), w["attn_sink"],
                    seq=seq)
    y = _mixffn(t_arr, x1.reshape(b, n2, n1, D_MODEL), ya.reshape(b, n2, n1, ATTN_WIDTH), d2, w["w_out"],
                w["g_ffn2"], w["w2_gate"], w["w2_up"], w["w2_down"], w["g_final"], n1=n1, n2=n2)
    return y.reshape(b, seq, D_MODEL)


def kernel(x_prompt, x_sample, g_ffn1, w1_gate, w1_up, w1_down, g_mix, w_in, w_fourier, attn_sink, w_out,
           g_ffn2, w2_gate, w2_up, w2_down, g_final):
    assert g_ffn1.shape[0] == 1, "single-layer trunk"
    w = {
        "g_ffn1": g_ffn1[0][None, :], "g_mix": g_mix[0][None, :], "g_ffn2": g_ffn2[0][None, :],
        "g_final": g_final[None, :],
        "w1_gate": w1_gate[0].astype(BF16), "w1_up": w1_up[0].astype(BF16), "w1_down": w1_down[0].astype(BF16),
        "w2_gate": w2_gate[0].astype(BF16), "w2_up": w2_up[0].astype(BF16), "w2_down": w2_down[0].astype(BF16),
        "w_proj": _fold_weights(w_in[0], w_fourier[0]),
        "w_out": w_out[0].astype(BF16),
        "attn_sink": attn_sink[0],
    }
    y_prompt = _trunk(x_prompt, w, n1=128, n2=128)
    y_sample = _trunk(x_sample, w, n1=64, n2=64)
    return (y_prompt, y_sample)
```

```python
import functools

import numpy as np
import jax
import jax.numpy as jnp
from jax import lax
from jax.experimental import pallas as pl
from jax.experimental.pallas import tpu as pltpu

D_MODEL = 1024
HEAD_DIM = 64
N_FOURIER_GROUPS = 8
FOURIER_WIDTH = N_FOURIER_GROUPS * HEAD_DIM
N_Q_HEADS = 8
N_KV_HEADS = 2
Q_PER_KV = N_Q_HEADS // N_KV_HEADS
ATTN_WIDTH = N_Q_HEADS * HEAD_DIM
KV_WIDTH = N_KV_HEADS * HEAD_DIM
D_FF = 2816
WINDOW = 128
BLOCK = 128
ROPE_THETA = 10000.0
RMS_EPS = 1e-6
NEG_INF = -1e30

SUBLANES = 8
LANES = 128
PROJ_WIDTH = 2 * FOURIER_WIDTH + ATTN_WIDTH + 2 * KV_WIDTH
VMEM_LIMIT_BYTES = 56 * 1024 * 1024
TOKEN_TILE = 512

F32 = jnp.float32
BF16 = jnp.bfloat16


def _rms(x, g):
    return x * lax.rsqrt(jnp.mean(x * x, axis=-1, keepdims=True) + RMS_EPS) * g


def _const_spec(shape):
    zeros = (0,) * len(shape)
    return pl.BlockSpec(shape, lambda *_: zeros, pipeline_mode=pl.Buffered(1))


def _params(n_axes):
    return pltpu.CompilerParams(dimension_semantics=("arbitrary",) * n_axes,
                                vmem_limit_bytes=VMEM_LIMIT_BYTES)


def _fold_kernel(w_in_ref, wf_ref, cc_ref, sc_ref, o_ref):
    hi = lax.Precision.HIGHEST
    for g in range(N_FOURIER_GROUPS):
        wf = wf_ref[g]
        pr = jnp.dot(cc_ref[...], wf, precision=hi, preferred_element_type=F32)
        pi = jnp.dot(sc_ref[...], wf, precision=hi, preferred_element_type=F32)
        wug = w_in_ref[:, g * HEAD_DIM:(g + 1) * HEAD_DIM]
        o_ref[:, g * HEAD_DIM:(g + 1) * HEAD_DIM] = jnp.dot(
            wug, pr, precision=hi, preferred_element_type=F32).astype(BF16)
        o_ref[:, FOURIER_WIDTH + g * HEAD_DIM:FOURIER_WIDTH + (g + 1) * HEAD_DIM] = (-jnp.dot(
            wug, pi, precision=hi, preferred_element_type=F32)).astype(BF16)
    o_ref[:, 2 * FOURIER_WIDTH:] = w_in_ref[:, FOURIER_WIDTH:].astype(BF16)


def _fold_weights(w_in, w_fourier):
    c = np.arange(HEAD_DIM)
    ang = 2.0 * np.pi * np.outer(c, c) / HEAD_DIM
    scale = HEAD_DIM ** -0.5
    cc = jnp.asarray(np.cos(ang) * scale, F32)
    sc = jnp.asarray(np.sin(ang) * scale, F32)
    return pl.pallas_call(
        _fold_kernel,
        out_shape=jax.ShapeDtypeStruct((D_MODEL, PROJ_WIDTH), BF16),
        name="fold",
    )(w_in, w_fourier, cc, sc)


FF_CHUNKS = ((0, 1024), (1024, 2048), (2048, D_FF))


def _swiglu(xn, wg_ref, wu_ref, wd_ref):
    acc = None
    for lo, hi in FF_CHUNKS:
        gate = jnp.dot(xn, wg_ref[:, lo:hi], preferred_element_type=F32)
        up = jnp.dot(xn, wu_ref[:, lo:hi], preferred_element_type=F32)
        act = (gate * jax.nn.sigmoid(gate) * up).astype(BF16)
        part = jnp.dot(act, wd_ref[lo:hi, :], preferred_element_type=F32)
        acc = part if acc is None else acc + part
    return acc


def _ffn_kernel(x_ref, g_ref, wg_ref, wu_ref, wd_ref, o_ref):
    half = x_ref.shape[0] // 2
    for h in range(2):
        rows = slice(h * half, (h + 1) * half)
        x = x_ref[rows, :]
        xn = _rms(x, g_ref[...]).astype(BF16)
        gate = jnp.dot(xn, wg_ref[...], preferred_element_type=F32)
        up = jnp.dot(xn, wu_ref[...], preferred_element_type=F32)
        act = (gate * jax.nn.sigmoid(gate) * up).astype(BF16)
        o_ref[rows, :] = x + 0.5 * jnp.dot(act, wd_ref[...], preferred_element_type=F32)


def _ffn(x, g, wg, wu, wd):
    t = x.shape[0]
    tm = TOKEN_TILE
    return pl.pallas_call(
        _ffn_kernel,
        grid=(t // tm,),
        in_specs=[
            pl.BlockSpec((tm, D_MODEL), lambda i: (i, 0)),
            _const_spec((1, D_MODEL)),
            _const_spec((D_MODEL, D_FF)),
            _const_spec((D_MODEL, D_FF)),
            _const_spec((D_FF, D_MODEL)),
        ],
        out_specs=pl.BlockSpec((tm, D_MODEL), lambda i: (i, 0)),
        out_shape=jax.ShapeDtypeStruct((t, D_MODEL), F32),
        compiler_params=_params(1),
        name="ffn",
    )(x, g, wg, wu, wd)


def _inproj_kernel(x_ref, g_ref, w_ref, ra_ref, rb_ref, ta_ref, tb_ref, d1_ref,
                   t_ref, q_ref, k_ref, v_ref, z_scr, *, n1):
    n1h = n1 // 2
    tmh = n1h * SUBLANES
    lane = lax.broadcasted_iota(jnp.int32, (tmh, LANES), 1)
    first_half = (lane % HEAD_DIM) < (HEAD_DIM // 2)
    q0 = 2 * FOURIER_WIDTH
    k0 = q0 + ATTN_WIDTH
    for h in range(2):
        s1s = slice(h * n1h, (h + 1) * n1h)
        x = x_ref[s1s].reshape(tmh, D_MODEL)
        hb = _rms(x, g_ref[...]).astype(BF16)
        proj = jnp.dot(hb, w_ref[...], preferred_element_type=F32)
        for c in range(2 * FOURIER_WIDTH // LANES):
            z_scr[c, h * tmh:(h + 1) * tmh, :] = proj[:, c * LANES:(c + 1) * LANES]

        def rows_a(t):
            return jnp.broadcast_to(t[:, None, :], (n1h, SUBLANES, LANES)).reshape(tmh, LANES)

        def rows_b(t):
            return jnp.broadcast_to(t[None, :, :], (n1h, SUBLANES, LANES)).reshape(tmh, LANES)

        ca, sa = rows_a(ra_ref[0, s1s, :]), rows_a(ra_ref[1, s1s, :])
        cos = ca * rows_b(rb_ref[0]) - sa * rows_b(rb_ref[1])
        sin = sa * rows_b(rb_ref[2]) + ca * rows_b(rb_ref[3])

        def rope(xc):
            rot = jnp.where(first_half, pltpu.roll(xc, LANES - HEAD_DIM // 2, 1), pltpu.roll(xc, HEAD_DIM // 2, 1))
            return xc * cos + rot * sin

        for c in range(ATTN_WIDTH // LANES):
            qc = rope(proj[:, q0 + c * LANES:q0 + (c + 1) * LANES]) * (HEAD_DIM ** -0.5)
            q_ref[s1s, :, c * LANES:(c + 1) * LANES] = qc.reshape(n1h, SUBLANES, LANES)
        k_ref[s1s] = rope(proj[:, k0:k0 + KV_WIDTH]).reshape(n1h, SUBLANES, KV_WIDTH)
        v_ref[s1s] = proj[:, k0 + KV_WIDTH:k0 + 2 * KV_WIDTH].reshape(n1h, SUBLANES, KV_WIDTH)

    d1 = d1_ref[...]
    for j in range(SUBLANES):
        rows = pl.ds(j, n1, stride=SUBLANES)
        nch = FOURIER_WIDTH // LANES
        zr = jnp.concatenate([z_scr[c, rows, :] for c in range(nch)], axis=1)
        zi = jnp.concatenate([z_scr[nch + c, rows, :] for c in range(nch)], axis=1)
        st = jnp.concatenate([zr, zi], axis=0).astype(BF16)
        t = jnp.dot(d1, st, preferred_element_type=F32)
        tr, ti = t[:n1], t[n1:]
        twc = ta_ref[0] * tb_ref[0, j] - ta_ref[1] * tb_ref[1, j]
        tws = ta_ref[1] * tb_ref[0, j] + ta_ref[0] * tb_ref[1, j]
        c4 = jnp.concatenate([twc] * (FOURIER_WIDTH // LANES), axis=1)
        s4 = jnp.concatenate([tws] * (FOURIER_WIDTH // LANES), axis=1)
        t_ref[j, :n1, :] = tr * c4 + ti * s4
        t_ref[j, n1:, :] = ti * c4 - tr * s4


def _inproj(x1, g, w, rope_a, rope_b, tw_a, tw_b, d1, *, n1, n2):
    b = x1.shape[0]
    blk = lambda w_: pl.BlockSpec((None, n1, SUBLANES, w_), lambda bi, i: (bi, 0, i, 0))
    tab_b = pl.BlockSpec((4, SUBLANES, LANES), lambda bi, i: (0, i, 0))
    tw_a_spec = pl.BlockSpec((2, None, n1, LANES), lambda bi, i: (0, i, 0, 0))
    return pl.pallas_call(
        functools.partial(_inproj_kernel, n1=n1),
        grid=(b, n2 // SUBLANES),
        in_specs=[blk(D_MODEL), _const_spec((1, D_MODEL)), _const_spec((D_MODEL, PROJ_WIDTH)),
                  _const_spec((2, n1, LANES)), tab_b, tw_a_spec, _const_spec((2, SUBLANES, n1, LANES)),
                  _const_spec((2 * n1, 2 * n1))],
        out_specs=[
            pl.BlockSpec((None, SUBLANES, 2 * n1, FOURIER_WIDTH), lambda bi, i: (bi, i, 0, 0)),
            blk(ATTN_WIDTH), blk(KV_WIDTH), blk(KV_WIDTH),
        ],
        out_shape=[
            jax.ShapeDtypeStruct((b, n2, 2 * n1, FOURIER_WIDTH), F32),
            jax.ShapeDtypeStruct((b, n1, n2, ATTN_WIDTH), F32),
            jax.ShapeDtypeStruct((b, n1, n2, KV_WIDTH), F32),
            jax.ShapeDtypeStruct((b, n1, n2, KV_WIDTH), F32),
        ],
        scratch_shapes=[pltpu.VMEM((2 * FOURIER_WIDTH // LANES, n1 * SUBLANES, LANES), F32)],
        compiler_params=_params(2),
        name="inproj",
    )(x1, g, w, rope_a, rope_b, tw_a, tw_b, d1)


F32_MAX = float(np.finfo(np.float32).max)
GROUP_WIDTH = Q_PER_KV * HEAD_DIM
ATTN_SOFTMAX_LAG = 1
ATTN_OUTPUT_LAG = 1


def _attn_kernel(sink_ref, q_ref, kp_ref, kc_ref, kn_ref, vp_ref, vc_ref, vn_ref, o_ref, *, nb, nblk_seq):
    i = pl.program_id(0)
    rows = (nb + 2) * BLOCK
    kfull = jnp.concatenate([kp_ref[...], kc_ref[...], kn_ref[...]], axis=0)
    vfull = jnp.concatenate([vp_ref[...], vc_ref[...], vn_ref[...]], axis=0)
    lo = lax.broadcasted_iota(jnp.int32, (rows, LANES), 1) < HEAD_DIM

    def dup(x):
        r = pltpu.roll(x, HEAD_DIM, 1)
        return jnp.where(lo, x, r), jnp.where(lo, r, x)

    kt = jnp.transpose(kfull).astype(BF16)
    zero_kt = jnp.zeros((HEAD_DIM, 3 * BLOCK), BF16)
    vv = dup(vfull)
    wkeys = 3 * BLOCK
    grp = lax.broadcasted_iota(jnp.int32, (wkeys, GROUP_WIDTH), 1) // HEAD_DIM
    ogrp = lax.broadcasted_iota(jnp.int32, (BLOCK, GROUP_WIDTH), 1) // HEAD_DIM
    kidx = lax.broadcasted_iota(jnp.int32, (BLOCK, BLOCK), 1)
    qidx = lax.broadcasted_iota(jnp.int32, (BLOCK, BLOCK), 0)
    cap_first = jnp.where(kidx >= qidx, F32_MAX, NEG_INF)
    cap_last = jnp.where(kidx <= qidx, F32_MAX, NEG_INF)

    def blockdiag(x):
        wide = jnp.concatenate([x, x], axis=1)
        return jnp.concatenate([jnp.where(grp == g, wide, 0.0) for g in range(Q_PER_KV)], axis=0).astype(BF16)

    def blockdiag_t(xt):
        return jnp.concatenate(
            [jnp.concatenate([xt if gg == g else zero_kt for gg in range(Q_PER_KV)], axis=1)
             for g in range(Q_PER_KV)], axis=0)

    caps = []
    for n in range(nb):
        pb = (i * nb + n) % nblk_seq
        caps.append((jnp.minimum(cap_first, jnp.where(pb > 0, F32_MAX, NEG_INF)),
                     jnp.minimum(cap_last, jnp.where(pb < nblk_seq - 1, F32_MAX, NEG_INF))))

    def scores(n, kh):
        kbd_t = blockdiag_t(kt[kh * HEAD_DIM:(kh + 1) * HEAD_DIM, n * BLOCK:n * BLOCK + wkeys])
        c0 = kh * GROUP_WIDTH
        qh = q_ref[n * BLOCK:(n + 1) * BLOCK, c0:c0 + GROUP_WIDTH].astype(BF16)
        return jnp.dot(qh, kbd_t, preferred_element_type=F32)

    def softmax(n, kh, s_all):
        capf, capl = caps[n]
        ps = []
        scale = None
        for g in range(Q_PER_KV - 1, -1, -1):
            s0 = jnp.minimum(s_all[:, g * wkeys:g * wkeys + BLOCK], capf)
            s1 = s_all[:, g * wkeys + BLOCK:g * wkeys + 2 * BLOCK]
            s2 = jnp.minimum(s_all[:, g * wkeys + 2 * BLOCK:(g + 1) * wkeys], capl)
            sink = sink_ref[kh * Q_PER_KV + g]
            m = jnp.max(jnp.maximum(jnp.maximum(s0, s1), s2), axis=-1, keepdims=True)
            p = [jnp.exp((t - m).astype(BF16)) for t in (s0, s1, s2)]
            psum = (p[0] + p[1] + p[2]).astype(F32)
            den = jnp.sum(psum, axis=-1, keepdims=True) + jnp.exp(sink - m)
            rinv = 1.0 / den
            scale = rinv if scale is None else jnp.where(ogrp == g, rinv, scale)
            ps = p + ps
        return jnp.concatenate(ps, axis=1), scale

    def output(n, kh, p_all, scale):
        vbd = blockdiag(vv[kh][n * BLOCK:n * BLOCK + wkeys])
        c0 = kh * GROUP_WIDTH
        o_ref[n * BLOCK:(n + 1) * BLOCK, c0:c0 + GROUP_WIDTH] = jnp.dot(
            p_all, vbd, preferred_element_type=F32) * scale

    units = [(n, kh) for n in range(nb) for kh in range(N_KV_HEADS)]
    s_vals, p_vals = {}, {}
    for step in range(len(units) + ATTN_OUTPUT_LAG):
        if step < len(units):
            s_vals[step] = scores(*units[step])
        u = step - ATTN_SOFTMAX_LAG
        if 0 <= u < len(units):
            p_vals[u] = softmax(*units[u], s_vals.pop(u))
        u = step - ATTN_OUTPUT_LAG
        if 0 <= u < len(units):
            output(*units[u], *p_vals.pop(u))


def _attention(q, k, v, sink, *, seq, nb=8):
    t = q.shape[0]
    nblk = t // BLOCK
    tq = nb * BLOCK
    cur = lambda w_: pl.BlockSpec((tq, w_), lambda i: (i, 0))
    prev = pl.BlockSpec((BLOCK, KV_WIDTH), lambda i: (jnp.maximum(i * nb - 1, 0), 0))
    nxt = pl.BlockSpec((BLOCK, KV_WIDTH), lambda i: (jnp.minimum(i * nb + nb, nblk - 1), 0))
    return pl.pallas_call(
        functools.partial(_attn_kernel, nb=nb, nblk_seq=seq // BLOCK),
        grid=(t // tq,),
        in_specs=[pl.BlockSpec(memory_space=pltpu.SMEM), cur(ATTN_WIDTH),
                  prev, cur(KV_WIDTH), nxt, prev, cur(KV_WIDTH), nxt],
        out_specs=cur(ATTN_WIDTH),
        out_shape=jax.ShapeDtypeStruct((t, ATTN_WIDTH), F32),
        compiler_params=_params(1),
        name="attention",
    )(sink, q, k, k, k, v, v, v)


def _mixffn_kernel(tr_ref, ti_ref, x_ref, ya_ref, d2_ref, wo_ref, g2_ref, wg_ref, wu_ref, wd_ref, gf_ref,
                   o_ref, tr_scr, ti_scr, yf_scr, *, n2, n2h):
    tm = n2h * SUBLANES
    nch = FOURIER_WIDTH // LANES

    @pl.when(pl.program_id(2) == 0)
    def _():
        for c in range(nch):
            tr_scr[c] = tr_ref[:, :, c * LANES:(c + 1) * LANES].reshape(n2 * SUBLANES, LANES)
            ti_scr[c] = ti_ref[:, :, c * LANES:(c + 1) * LANES].reshape(n2 * SUBLANES, LANES)

    d2 = d2_ref[...]
    for j in range(SUBLANES):
        rows = pl.ds(j, n2, stride=SUBLANES)
        tr = jnp.concatenate([tr_scr[c, rows, :] for c in range(nch)], axis=1)
        ti = jnp.concatenate([ti_scr[c, rows, :] for c in range(nch)], axis=1)
        st = jnp.concatenate([tr, ti], axis=0).astype(BF16)
        yfj = jnp.dot(d2, st, preferred_element_type=F32)
        for c in range(nch):
            yf_scr[c, pl.ds(j, n2h, stride=SUBLANES), :] = yfj[:, c * LANES:(c + 1) * LANES]
    yf = jnp.concatenate([yf_scr[c] for c in range(nch)], axis=1).astype(BF16)
    ya = ya_ref[...].reshape(tm, ATTN_WIDTH).astype(BF16)
    x2 = (x_ref[...].reshape(tm, D_MODEL)
          + jnp.dot(yf, wo_ref[:FOURIER_WIDTH, :], preferred_element_type=F32)
          + jnp.dot(ya, wo_ref[FOURIER_WIDTH:, :], preferred_element_type=F32))

    half = tm // 2
    for h in range(2):
        xh = x2[h * half:(h + 1) * half]
        xn = _rms(xh, g2_ref[...]).astype(BF16)
        y = _rms(xh + 0.5 * _swiglu(xn, wg_ref, wu_ref, wd_ref), gf_ref[...])
        o_ref[h * (n2h // 2):(h + 1) * (n2h // 2)] = y.reshape(n2h // 2, SUBLANES, D_MODEL)


def _mixffn(t_arr, x1, ya, d2, wo, g2, wg, wu, wd, gf, *, n1, n2):
    b = x1.shape[0]
    halves = (n2 * SUBLANES) // TOKEN_TILE
    n2h = n2 // halves
    t_re = pl.BlockSpec((None, n2, SUBLANES, FOURIER_WIDTH), lambda bi, i, h: (bi, 0, i, 0))
    t_im = pl.BlockSpec((None, n2, SUBLANES, FOURIER_WIDTH), lambda bi, i, h: (bi, 0, n1 // SUBLANES + i, 0))
    blk = lambda w_: pl.BlockSpec((None, n2h, SUBLANES, w_), lambda bi, i, h: (bi, h, i, 0))
    return pl.pallas_call(
        functools.partial(_mixffn_kernel, n2=n2, n2h=n2h),
        grid=(b, n1 // SUBLANES, halves),
        in_specs=[t_re, t_im, blk(D_MODEL), blk(ATTN_WIDTH),
                  pl.BlockSpec((n2h, 2 * n2), lambda bi, i, h: (h, 0)),
                  _const_spec((2 * FOURIER_WIDTH, D_MODEL)), _const_spec((1, D_MODEL)),
                  _const_spec((D_MODEL, D_FF)), _const_spec((D_MODEL, D_FF)), _const_spec((D_FF, D_MODEL)),
                  _const_spec((1, D_MODEL))],
        out_specs=blk(D_MODEL),
        out_shape=jax.ShapeDtypeStruct((b, n2, n1, D_MODEL), F32),
        scratch_shapes=[pltpu.VMEM((FOURIER_WIDTH // LANES, n2 * SUBLANES, LANES), F32)] * 2
        + [pltpu.VMEM((FOURIER_WIDTH // LANES, n2h * SUBLANES, LANES), F32)],
        compiler_params=_params(3),
        name="mixffn",
    )(t_arr, t_arr, x1, ya, d2, wo, g2, wg, wu, wd, gf)


def _dft_tables(n1, n2):
    s = n1 * n2
    a1 = 2.0 * np.pi * np.outer(np.arange(n1), np.arange(n1)) / n1
    c1, s1 = np.cos(a1) / np.sqrt(n1), np.sin(a1) / np.sqrt(n1)
    d1 = np.block([[c1, s1], [-s1, c1]])
    a2 = 2.0 * np.pi * np.outer(np.arange(n2), np.arange(n2)) / n2
    d2 = np.concatenate([np.cos(a2), np.sin(a2)], axis=1) / np.sqrt(n2)
    k1 = jnp.arange(n1, dtype=jnp.int32)[None, :]

    def table(mult, count):
        m = (jnp.arange(count, dtype=jnp.int32)[:, None] * mult * k1) % s
        ang = m.astype(F32) * (2.0 * np.pi / s)
        return jnp.broadcast_to(jnp.stack([jnp.cos(ang), jnp.sin(ang)])[..., None], (2, count, n1, LANES))

    tw_a, tw_b = table(SUBLANES, n2 // SUBLANES), table(1, SUBLANES)
    return jnp.asarray(d1, F32).astype(BF16), jnp.asarray(d2, F32).astype(BF16), tw_a, tw_b


def _rope_tables(n1, n2):
    half = HEAD_DIM // 2
    lane = np.arange(LANES)
    inv_freq = ROPE_THETA ** (-(lane % half) / half)
    ang_a = (np.arange(n1) * n2)[:, None] * inv_freq[None, :]
    ang_b = np.arange(n2)[:, None] * inv_freq[None, :]
    sign = np.where((lane % HEAD_DIM) < half, -1.0, 1.0)[None, :]
    rope_a = np.stack([np.cos(ang_a), np.sin(ang_a)])
    rope_b = np.stack([np.cos(ang_b), np.sin(ang_b), sign * np.cos(ang_b), sign * np.sin(ang_b)])
    return jnp.asarray(rope_a, F32), jnp.asarray(rope_b, F32)


def _trunk(x, w, *, n1, n2):
    b, seq, _ = x.shape
    t = b * seq
    d1, d2, tw_a, tw_b = _dft_tables(n1, n2)
    rope_a, rope_b = _rope_tables(n1, n2)
    x1 = _ffn(x.reshape(t, D_MODEL), w["g_ffn1"], w["w1_gate"], w["w1_up"], w["w1_down"])
    t_arr, q, k, v = _inproj(x1.reshape(b, n1, n2, D_MODEL), w["g_mix"], w["w_proj"], rope_a, rope_b,
                             tw_a, tw_b, d1, n1=n1, n2=n2)
    ya = _attention(q.reshape(t, ATTN_WIDTH), k.reshape(t, KV_WIDTH), v.reshape(t, KV_WIDTH), w["attn_sink"],
                    seq=seq)
    y = _mixffn(t_arr, x1.reshape(b, n2, n1, D_MODEL), ya.reshape(b, n2, n1, ATTN_WIDTH), d2, w["w_out"],
                w["g_ffn2"], w["w2_gate"], w["w2_up"], w["w2_down"], w["g_final"], n1=n1, n2=n2)
    return y.reshape(b, seq, D_MODEL)


def kernel(x_prompt, x_sample, g_ffn1, w1_gate, w1_up, w1_down, g_mix, w_in, w_fourier, attn_sink, w_out,
           g_ffn2, w2_gate, w2_up, w2_down, g_final):
    assert g_ffn1.shape[0] == 1, "single-layer trunk"
    w = {
        "g_ffn1": g_ffn1[0][None, :], "g_mix": g_mix[0][None, :], "g_ffn2": g_ffn2[0][None, :],
        "g_final": g_final[None, :],
        "w1_gate": w1_gate[0].astype(BF16), "w1_up": w1_up[0].astype(BF16), "w1_down": w1_down[0].astype(BF16),
        "w2_gate": w2_gate[0].astype(BF16), "w2_up": w2_up[0].astype(BF16), "w2_down": w2_down[0].astype(BF16),
        "w_proj": _fold_weights(w_in[0], w_fourier[0]),
        "w_out": w_out[0].astype(BF16),
        "attn_sink": attn_sink[0],
    }
    y_prompt = _trunk(x_prompt, w, n1=128, n2=128)
    y_sample = _trunk(x_sample, w, n1=64, n2=64)
    return (y_prompt, y_sample)
```

```python
import functools

import numpy as np
import jax
import jax.numpy as jnp
from jax import lax
from jax.experimental import pallas as pl
from jax.experimental.pallas import tpu as pltpu

D_MODEL = 1024
HEAD_DIM = 64
N_FOURIER_GROUPS = 8
FOURIER_WIDTH = N_FOURIER_GROUPS * HEAD_DIM
N_Q_HEADS = 8
N_KV_HEADS = 2
Q_PER_KV = N_Q_HEADS // N_KV_HEADS
ATTN_WIDTH = N_Q_HEADS * HEAD_DIM
KV_WIDTH = N_KV_HEADS * HEAD_DIM
D_FF = 2816
WINDOW = 128
BLOCK = 128
ROPE_THETA = 10000.0
RMS_EPS = 1e-6
NEG_INF = -1e30

SUBLANES = 8
LANES = 128
PROJ_WIDTH = 2 * FOURIER_WIDTH + ATTN_WIDTH + 2 * KV_WIDTH
VMEM_LIMIT_BYTES = 56 * 1024 * 1024
TOKEN_TILE = 512

F32 = jnp.float32
BF16 = jnp.bfloat16


def _rms(x, g):
    return x * lax.rsqrt(jnp.mean(x * x, axis=-1, keepdims=True) + RMS_EPS) * g


def _const_spec(shape):
    zeros = (0,) * len(shape)
    return pl.BlockSpec(shape, lambda *_: zeros, pipeline_mode=pl.Buffered(1))


def _params(n_axes):
    return pltpu.CompilerParams(dimension_semantics=("arbitrary",) * n_axes,
                                vmem_limit_bytes=VMEM_LIMIT_BYTES)


def _fold_kernel(w_in_ref, wf_ref, cc_ref, sc_ref, o_ref):
    hi = lax.Precision.HIGHEST
    for g in range(N_FOURIER_GROUPS):
        wf = wf_ref[g]
        pr = jnp.dot(cc_ref[...], wf, precision=hi, preferred_element_type=F32)
        pi = jnp.dot(sc_ref[...], wf, precision=hi, preferred_element_type=F32)
        wug = w_in_ref[:, g * HEAD_DIM:(g + 1) * HEAD_DIM]
        o_ref[:, g * HEAD_DIM:(g + 1) * HEAD_DIM] = jnp.dot(
            wug, pr, precision=hi, preferred_element_type=F32).astype(BF16)
        o_ref[:, FOURIER_WIDTH + g * HEAD_DIM:FOURIER_WIDTH + (g + 1) * HEAD_DIM] = (-jnp.dot(
            wug, pi, precision=hi, preferred_element_type=F32)).astype(BF16)
    o_ref[:, 2 * FOURIER_WIDTH:] = w_in_ref[:, FOURIER_WIDTH:].astype(BF16)


def _fold_weights(w_in, w_fourier):
    c = np.arange(HEAD_DIM)
    ang = 2.0 * np.pi * np.outer(c, c) / HEAD_DIM
    scale = HEAD_DIM ** -0.5
    cc = jnp.asarray(np.cos(ang) * scale, F32)
    sc = jnp.asarray(np.sin(ang) * scale, F32)
    return pl.pallas_call(
        _fold_kernel,
        out_shape=jax.ShapeDtypeStruct((D_MODEL, PROJ_WIDTH), BF16),
        name="fold",
    )(w_in, w_fourier, cc, sc)


FF_CHUNKS = ((0, 1024), (1024, 2048), (2048, D_FF))


def _swiglu(xn, wg_ref, wu_ref, wd_ref):
    acc = None
    for lo, hi in FF_CHUNKS:
        gate = jnp.dot(xn, wg_ref[:, lo:hi], preferred_element_type=F32)
        up = jnp.dot(xn, wu_ref[:, lo:hi], preferred_element_type=F32)
        act = (gate * jax.nn.sigmoid(gate) * up).astype(BF16)
        part = jnp.dot(act, wd_ref[lo:hi, :], preferred_element_type=F32)
        acc = part if acc is None else acc + part
    return acc


def _ffn_kernel(x_ref, g_ref, wg_ref, wu_ref, wd_ref, o_ref):
    half = x_ref.shape[0] // 2
    for h in range(2):
        rows = slice(h * half, (h + 1) * half)
        x = x_ref[rows, :]
        xn = _rms(x, g_ref[...]).astype(BF16)
        gate = jnp.dot(xn, wg_ref[...], preferred_element_type=F32)
        up = jnp.dot(xn, wu_ref[...], preferred_element_type=F32)
        act = (gate * jax.nn.sigmoid(gate) * up).astype(BF16)
        o_ref[rows, :] = x + 0.5 * jnp.dot(act, wd_ref[...], preferred_element_type=F32)


def _ffn(x, g, wg, wu, wd):
    t = x.shape[0]
    tm = TOKEN_TILE
    return pl.pallas_call(
        _ffn_kernel,
        grid=(t // tm,),
        in_specs=[
            pl.BlockSpec((tm, D_MODEL), lambda i: (i, 0)),
            _const_spec((1, D_MODEL)),
            _const_spec((D_MODEL, D_FF)),
            _const_spec((D_MODEL, D_FF)),
            _const_spec((D_FF, D_MODEL)),
        ],
        out_specs=pl.BlockSpec((tm, D_MODEL), lambda i: (i, 0)),
        out_shape=jax.ShapeDtypeStruct((t, D_MODEL), F32),
        compiler_params=_params(1),
        name="ffn",
    )(x, g, wg, wu, wd)


def _inproj_kernel(x_ref, g_ref, w_ref, ra_ref, rb_ref, ta_ref, tb_ref, d1_ref,
                   t_ref, q_ref, k_ref, v_ref, z_scr, *, n1):
    n1h = n1 // 2
    tmh = n1h * SUBLANES
    lane = lax.broadcasted_iota(jnp.int32, (tmh, LANES), 1)
    first_half = (lane % HEAD_DIM) < (HEAD_DIM // 2)
    q0 = 2 * FOURIER_WIDTH
    k0 = q0 + ATTN_WIDTH
    for h in range(2):
        s1s = slice(h * n1h, (h + 1) * n1h)
        x = x_ref[s1s].reshape(tmh, D_MODEL)
        hb = _rms(x, g_ref[...]).astype(BF16)
        proj = jnp.dot(hb, w_ref[...], preferred_element_type=F32)
        for c in range(2 * FOURIER_WIDTH // LANES):
            z_scr[c, h * tmh:(h + 1) * tmh, :] = proj[:, c * LANES:(c + 1) * LANES]

        def rows_a(t):
            return jnp.broadcast_to(t[:, None, :], (n1h, SUBLANES, LANES)).reshape(tmh, LANES)

        def rows_b(t):
            return jnp.broadcast_to(t[None, :, :], (n1h, SUBLANES, LANES)).reshape(tmh, LANES)

        ca, sa = rows_a(ra_ref[0, s1s, :]), rows_a(ra_ref[1, s1s, :])
        cos = ca * rows_b(rb_ref[0]) - sa * rows_b(rb_ref[1])
        sin = sa * rows_b(rb_ref[2]) + ca * rows_b(rb_ref[3])

        def rope(xc):
            rot = jnp.where(first_half, pltpu.roll(xc, LANES - HEAD_DIM // 2, 1), pltpu.roll(xc, HEAD_DIM // 2, 1))
            return xc * cos + rot * sin

        for c in range(ATTN_WIDTH // LANES):
            qc = rope(proj[:, q0 + c * LANES:q0 + (c + 1) * LANES]) * (HEAD_DIM ** -0.5)
            q_ref[s1s, :, c * LANES:(c + 1) * LANES] = qc.reshape(n1h, SUBLANES, LANES)
        k_ref[s1s] = rope(proj[:, k0:k0 + KV_WIDTH]).reshape(n1h, SUBLANES, KV_WIDTH)
        v_ref[s1s] = proj[:, k0 + KV_WIDTH:k0 + 2 * KV_WIDTH].reshape(n1h, SUBLANES, KV_WIDTH)

    d1 = d1_ref[...]
    for j in range(SUBLANES):
        rows = pl.ds(j, n1, stride=SUBLANES)
        nch = FOURIER_WIDTH // LANES
        zr = jnp.concatenate([z_scr[c, rows, :] for c in range(nch)], axis=1)
        zi = jnp.concatenate([z_scr[nch + c, rows, :] for c in range(nch)], axis=1)
        st = jnp.concatenate([zr, zi], axis=0).astype(BF16)
        t = jnp.dot(d1, st, preferred_element_type=F32)
        tr, ti = t[:n1], t[n1:]
        twc = ta_ref[0] * tb_ref[0, j] - ta_ref[1] * tb_ref[1, j]
        tws = ta_ref[1] * tb_ref[0, j] + ta_ref[0] * tb_ref[1, j]
        c4 = jnp.concatenate([twc] * (FOURIER_WIDTH // LANES), axis=1)
        s4 = jnp.concatenate([tws] * (FOURIER_WIDTH // LANES), axis=1)
        t_ref[j, :n1, :] = tr * c4 + ti * s4
        t_ref[j, n1:, :] = ti * c4 - tr * s4


def _inproj(x1, g, w, rope_a, rope_b, tw_a, tw_b, d1, *, n1, n2):
    b = x1.shape[0]
    blk = lambda w_: pl.BlockSpec((None, n1, SUBLANES, w_), lambda bi, i: (bi, 0, i, 0))
    tab_b = pl.BlockSpec((4, SUBLANES, LANES), lambda bi, i: (0, i, 0))
    tw_a_spec = pl.BlockSpec((2, None, n1, LANES), lambda bi, i: (0, i, 0, 0))
    return pl.pallas_call(
        functools.partial(_inproj_kernel, n1=n1),
        grid=(b, n2 // SUBLANES),
        in_specs=[blk(D_MODEL), _const_spec((1, D_MODEL)), _const_spec((D_MODEL, PROJ_WIDTH)),
                  _const_spec((2, n1, LANES)), tab_b, tw_a_spec, _const_spec((2, SUBLANES, n1, LANES)),
                  _const_spec((2 * n1, 2 * n1))],
        out_specs=[
            pl.BlockSpec((None, SUBLANES, 2 * n1, FOURIER_WIDTH), lambda bi, i: (bi, i, 0, 0)),
            blk(ATTN_WIDTH), blk(KV_WIDTH), blk(KV_WIDTH),
        ],
        out_shape=[
            jax.ShapeDtypeStruct((b, n2, 2 * n1, FOURIER_WIDTH), F32),
            jax.ShapeDtypeStruct((b, n1, n2, ATTN_WIDTH), F32),
            jax.ShapeDtypeStruct((b, n1, n2, KV_WIDTH), F32),
            jax.ShapeDtypeStruct((b, n1, n2, KV_WIDTH), F32),
        ],
        scratch_shapes=[pltpu.VMEM((2 * FOURIER_WIDTH // LANES, n1 * SUBLANES, LANES), F32)],
        compiler_params=_params(2),
        name="inproj",
    )(x1, g, w, rope_a, rope_b, tw_a, tw_b, d1)


F32_MAX = float(np.finfo(np.float32).max)
GROUP_WIDTH = Q_PER_KV * HEAD_DIM


def _attn_kernel(sink_ref, q_ref, kp_ref, kc_ref, kn_ref, vp_ref, vc_ref, vn_ref, o_ref, *, nb, nblk_seq):
    i = pl.program_id(0)
    rows = (nb + 2) * BLOCK
    kfull = jnp.concatenate([kp_ref[...], kc_ref[...], kn_ref[...]], axis=0)
    vfull = jnp.concatenate([vp_ref[...], vc_ref[...], vn_ref[...]], axis=0)
    lo = lax.broadcasted_iota(jnp.int32, (rows, LANES), 1) < HEAD_DIM

    def dup(x):
        r = pltpu.roll(x, HEAD_DIM, 1)
        return jnp.where(lo, x, r), jnp.where(lo, r, x)

    kt = jnp.transpose(kfull).astype(BF16)
    zero_kt = jnp.zeros((HEAD_DIM, 3 * BLOCK), BF16)
    vv = dup(vfull)
    wkeys = 3 * BLOCK
    grp = lax.broadcasted_iota(jnp.int32, (wkeys, GROUP_WIDTH), 1) // HEAD_DIM
    ogrp = lax.broadcasted_iota(jnp.int32, (BLOCK, GROUP_WIDTH), 1) // HEAD_DIM
    kidx = lax.broadcasted_iota(jnp.int32, (BLOCK, BLOCK), 1)
    qidx = lax.broadcasted_iota(jnp.int32, (BLOCK, BLOCK), 0)
    cap_first = jnp.where(kidx >= qidx, F32_MAX, NEG_INF)
    cap_last = jnp.where(kidx <= qidx, F32_MAX, NEG_INF)

    def blockdiag(x):
        wide = jnp.concatenate([x, x], axis=1)
        return jnp.concatenate([jnp.where(grp == g, wide, 0.0) for g in range(Q_PER_KV)], axis=0).astype(BF16)

    def blockdiag_t(xt):
        return jnp.concatenate(
            [jnp.concatenate([xt if gg == g else zero_kt for gg in range(Q_PER_KV)], axis=1)
             for g in range(Q_PER_KV)], axis=0)

    caps = []
    for n in range(nb):
        pb = (i * nb + n) % nblk_seq
        caps.append((jnp.minimum(cap_first, jnp.where(pb > 0, F32_MAX, NEG_INF)),
                     jnp.minimum(cap_last, jnp.where(pb < nblk_seq - 1, F32_MAX, NEG_INF))))

    def scores(n, kh):
        kbd_t = blockdiag_t(kt[kh * HEAD_DIM:(kh + 1) * HEAD_DIM, n * BLOCK:n * BLOCK + wkeys])
        c0 = kh * GROUP_WIDTH
        qh = q_ref[n * BLOCK:(n + 1) * BLOCK, c0:c0 + GROUP_WIDTH].astype(BF16)
        return jnp.dot(qh, kbd_t, preferred_element_type=F32)

    def softmax(n, kh, s_all):
        capf, capl = caps[n]
        tiles, maxes = [], []
        for g in range(Q_PER_KV):
            s0 = jnp.minimum(s_all[:, g * wkeys:g * wkeys + BLOCK], capf)
            s1 = s_all[:, g * wkeys + BLOCK:g * wkeys + 2 * BLOCK]
            s2 = jnp.minimum(s_all[:, g * wkeys + 2 * BLOCK:(g + 1) * wkeys], capl)
            tiles.append((s0, s1, s2))
            maxes.append(jnp.max(jnp.maximum(jnp.maximum(s0, s1), s2), axis=-1, keepdims=True))
        ps, dens = [], []
        for g in range(Q_PER_KV):
            p = [jnp.exp((t - maxes[g]).astype(BF16)) for t in tiles[g]]
            psum = (p[0] + p[1] + p[2]).astype(F32)
            dens.append(jnp.sum(psum, axis=-1, keepdims=True) + jnp.exp(sink_ref[kh * Q_PER_KV + g] - maxes[g]))
            ps += p
        scale = 1.0 / dens[Q_PER_KV - 1]
        for g in range(Q_PER_KV - 2, -1, -1):
            scale = jnp.where(ogrp == g, 1.0 / dens[g], scale)
        return jnp.concatenate(ps, axis=1), scale

    def output(n, kh, p_all, scale):
        vbd = blockdiag(vv[kh][n * BLOCK:n * BLOCK + wkeys])
        c0 = kh * GROUP_WIDTH
        o_ref[n * BLOCK:(n + 1) * BLOCK, c0:c0 + GROUP_WIDTH] = jnp.dot(
            p_all, vbd, preferred_element_type=F32) * scale

    units = [(n, kh) for kh in range(N_KV_HEADS) for n in range(nb)]
    s_prev = None
    for k in range(len(units) + 1):
        s_cur = scores(*units[k]) if k < len(units) else None
        if s_prev is not None:
            output(*units[k - 1], *softmax(*units[k - 1], s_prev))
        s_prev = s_cur


def _attention(q, k, v, sink, *, seq, nb=16):
    t = q.shape[0]
    nblk = t // BLOCK
    tq = nb * BLOCK
    cur = lambda w_: pl.BlockSpec((tq, w_), lambda i: (i, 0))
    prev = pl.BlockSpec((BLOCK, KV_WIDTH), lambda i: (jnp.maximum(i * nb - 1, 0), 0))
    nxt = pl.BlockSpec((BLOCK, KV_WIDTH), lambda i: (jnp.minimum(i * nb + nb, nblk - 1), 0))
    return pl.pallas_call(
        functools.partial(_attn_kernel, nb=nb, nblk_seq=seq // BLOCK),
        grid=(t // tq,),
        in_specs=[pl.BlockSpec(memory_space=pltpu.SMEM), cur(ATTN_WIDTH),
                  prev, cur(KV_WIDTH), nxt, prev, cur(KV_WIDTH), nxt],
        out_specs=cur(ATTN_WIDTH),
        out_shape=jax.ShapeDtypeStruct((t, ATTN_WIDTH), F32),
        compiler_params=_params(1),
        name="attention",
    )(sink, q, k, k, k, v, v, v)


def _mixffn_kernel(tr_ref, ti_ref, x_ref, ya_ref, d2_ref, wo_ref, g2_ref, wg_ref, wu_ref, wd_ref, gf_ref,
                   o_ref, tr_scr, ti_scr, yf_scr, *, n2, n2h):
    tm = n2h * SUBLANES
    nch = FOURIER_WIDTH // LANES

    @pl.when(pl.program_id(2) == 0)
    def _():
        for c in range(nch):
            tr_scr[c] = tr_ref[:, :, c * LANES:(c + 1) * LANES].reshape(n2 * SUBLANES, LANES)
            ti_scr[c] = ti_ref[:, :, c * LANES:(c + 1) * LANES].reshape(n2 * SUBLANES, LANES)

    d2 = d2_ref[...]
    for j in range(SUBLANES):
        rows = pl.ds(j, n2, stride=SUBLANES)
        tr = jnp.concatenate([tr_scr[c, rows, :] for c in range(nch)], axis=1)
        ti = jnp.concatenate([ti_scr[c, rows, :] for c in range(nch)], axis=1)
        st = jnp.concatenate([tr, ti], axis=0).astype(BF16)
        yfj = jnp.dot(d2, st, preferred_element_type=F32)
        for c in range(nch):
            yf_scr[c, pl.ds(j, n2h, stride=SUBLANES), :] = yfj[:, c * LANES:(c + 1) * LANES]
    yf = jnp.concatenate([yf_scr[c] for c in range(nch)], axis=1).astype(BF16)
    ya = ya_ref[...].reshape(tm, ATTN_WIDTH).astype(BF16)
    x2 = (x_ref[...].reshape(tm, D_MODEL)
          + jnp.dot(yf, wo_ref[:FOURIER_WIDTH, :], preferred_element_type=F32)
          + jnp.dot(ya, wo_ref[FOURIER_WIDTH:, :], preferred_element_type=F32))

    half = tm // 2
    for h in range(2):
        xh = x2[h * half:(h + 1) * half]
        xn = _rms(xh, g2_ref[...]).astype(BF16)
        y = _rms(xh + 0.5 * _swiglu(xn, wg_ref, wu_ref, wd_ref), gf_ref[...])
        o_ref[h * (n2h // 2):(h + 1) * (n2h // 2)] = y.reshape(n2h // 2, SUBLANES, D_MODEL)


def _mixffn(t_arr, x1, ya, d2, wo, g2, wg, wu, wd, gf, *, n1, n2):
    b = x1.shape[0]
    halves = (n2 * SUBLANES) // TOKEN_TILE
    n2h = n2 // halves
    t_re = pl.BlockSpec((None, n2, SUBLANES, FOURIER_WIDTH), lambda bi, i, h: (bi, 0, i, 0))
    t_im = pl.BlockSpec((None, n2, SUBLANES, FOURIER_WIDTH), lambda bi, i, h: (bi, 0, n1 // SUBLANES + i, 0))
    blk = lambda w_: pl.BlockSpec((None, n2h, SUBLANES, w_), lambda bi, i, h: (bi, h, i, 0))
    return pl.pallas_call(
        functools.partial(_mixffn_kernel, n2=n2, n2h=n2h),
        grid=(b, n1 // SUBLANES, halves),
        in_specs=[t_re, t_im, blk(D_MODEL), blk(ATTN_WIDTH),
                  pl.BlockSpec((n2h, 2 * n2), lambda bi, i, h: (h, 0)),
                  _const_spec((2 * FOURIER_WIDTH, D_MODEL)), _const_spec((1, D_MODEL)),
                  _const_spec((D_MODEL, D_FF)), _const_spec((D_MODEL, D_FF)), _const_spec((D_FF, D_MODEL)),
                  _const_spec((1, D_MODEL))],
        out_specs=blk(D_MODEL),
        out_shape=jax.ShapeDtypeStruct((b, n2, n1, D_MODEL), F32),
        scratch_shapes=[pltpu.VMEM((FOURIER_WIDTH // LANES, n2 * SUBLANES, LANES), F32)] * 2
        + [pltpu.VMEM((FOURIER_WIDTH // LANES, n2h * SUBLANES, LANES), F32)],
        compiler_params=_params(3),
        name="mixffn",
    )(t_arr, t_arr, x1, ya, d2, wo, g2, wg, wu, wd, gf)


def _dft_tables(n1, n2):
    s = n1 * n2
    a1 = 2.0 * np.pi * np.outer(np.arange(n1), np.arange(n1)) / n1
    c1, s1 = np.cos(a1) / np.sqrt(n1), np.sin(a1) / np.sqrt(n1)
    d1 = np.block([[c1, s1], [-s1, c1]])
    a2 = 2.0 * np.pi * np.outer(np.arange(n2), np.arange(n2)) / n2
    d2 = np.concatenate([np.cos(a2), np.sin(a2)], axis=1) / np.sqrt(n2)
    k1 = jnp.arange(n1, dtype=jnp.int32)[None, :]

    def table(mult, count):
        m = (jnp.arange(count, dtype=jnp.int32)[:, None] * mult * k1) % s
        ang = m.astype(F32) * (2.0 * np.pi / s)
        return jnp.broadcast_to(jnp.stack([jnp.cos(ang), jnp.sin(ang)])[..., None], (2, count, n1, LANES))

    tw_a, tw_b = table(SUBLANES, n2 // SUBLANES), table(1, SUBLANES)
    return jnp.asarray(d1, F32).astype(BF16), jnp.asarray(d2, F32).astype(BF16), tw_a, tw_b


def _rope_tables(n1, n2):
    half = HEAD_DIM // 2
    lane = np.arange(LANES)
    inv_freq = ROPE_THETA ** (-(lane % half) / half)
    ang_a = (np.arange(n1) * n2)[:, None] * inv_freq[None, :]
    ang_b = np.arange(n2)[:, None] * inv_freq[None, :]
    sign = np.where((lane % HEAD_DIM) < half, -1.0, 1.0)[None, :]
    rope_a = np.stack([np.cos(ang_a), np.sin(ang_a)])
    rope_b = np.stack([np.cos(ang_b), np.sin(ang_b), sign * np.cos(ang_b), sign * np.sin(ang_b)])
    return jnp.asarray(rope_a, F32), jnp.asarray(rope_b, F32)


def _trunk(x, w, *, n1, n2):
    b, seq, _ = x.shape
    t = b * seq
    d1, d2, tw_a, tw_b = _dft_tables(n1, n2)
    rope_a, rope_b = _rope_tables(n1, n2)
    x1 = _ffn(x.reshape(t, D_MODEL), w["g_ffn1"], w["w1_gate"], w["w1_up"], w["w1_down"])
    t_arr, q, k, v = _inproj(x1.reshape(b, n1, n2, D_MODEL), w["g_mix"], w["w_proj"], rope_a, rope_b,
                             tw_a, tw_b, d1, n1=n1, n2=n2)
    ya = _attention(q.reshape(t, ATTN_WIDTH), k.reshape(t, KV_WIDTH), v.reshape(t, KV_WIDTH), w["attn_sink"],
                    seq=seq)
    y = _mixffn(t_arr, x1.reshape(b, n2, n1, D_MODEL), ya.reshape(b, n2, n1, ATTN_WIDTH), d2, w["w_out"],
                w["g_ffn2"], w["w2_gate"], w["w2_up"], w["w2_down"], w["g_final"], n1=n1, n2=n2)
    return y.reshape(b, seq, D_MODEL)


def kernel(x_prompt, x_sample, g_ffn1, w1_gate, w1_up, w1_down, g_mix, w_in, w_fourier, attn_sink, w_out,
           g_ffn2, w2_gate, w2_up, w2_down, g_final):
    assert g_ffn1.shape[0] == 1, "single-layer trunk"
    w = {
        "g_ffn1": g_ffn1[0][None, :], "g_mix": g_mix[0][None, :], "g_ffn2": g_ffn2[0][None, :],
        "g_final": g_final[None, :],
        "w1_gate": w1_gate[0].astype(BF16), "w1_up": w1_up[0].astype(BF16), "w1_down": w1_down[0].astype(BF16),
        "w2_gate": w2_gate[0].astype(BF16), "w2_up": w2_up[0].astype(BF16), "w2_down": w2_down[0].astype(BF16),
        "w_proj": _fold_weights(w_in[0], w_fourier[0]),
        "w_out": w_out[0].astype(BF16),
        "attn_sink": attn_sink[0],
    }
    y_prompt = _trunk(x_prompt, w, n1=128, n2=128)
    y_sample = _trunk(x_sample, w, n1=64, n2=64)
    return (y_prompt, y_sample)
```

```python
import functools
import math

import numpy as np
import jax
import jax.numpy as jnp
from jax import lax
from jax.experimental import pallas as pl
from jax.experimental.pallas import tpu as pltpu

D_MODEL = 1024
HEAD_DIM = 64
N_FOURIER_GROUPS = 8
FOURIER_WIDTH = N_FOURIER_GROUPS * HEAD_DIM
N_Q_HEADS = 8
N_KV_HEADS = 2
Q_PER_KV = N_Q_HEADS // N_KV_HEADS
ATTN_WIDTH = N_Q_HEADS * HEAD_DIM
KV_WIDTH = N_KV_HEADS * HEAD_DIM
D_FF = 2816
WINDOW = 128
BLOCK = 128
ROPE_THETA = 10000.0
RMS_EPS = 1e-6
NEG_INF = -1e30

SUBLANES = 8
BF16_SUBLANES = 16
LANES = 128
PROJ_WIDTH = 2 * FOURIER_WIDTH + ATTN_WIDTH + 2 * KV_WIDTH
VMEM_LIMIT_BYTES = 56 * 1024 * 1024
TOKEN_TILE = 512
INPROJ_ROWS = 256

F32 = jnp.float32
BF16 = jnp.bfloat16


def _rms(x, g):
    return x * lax.rsqrt(jnp.mean(x * x, axis=-1, keepdims=True) + RMS_EPS) * g


def _const_spec(shape):
    zeros = (0,) * len(shape)
    return pl.BlockSpec(shape, lambda *_: zeros, pipeline_mode=pl.Buffered(1))


def _params(n_axes):
    return pltpu.CompilerParams(dimension_semantics=("arbitrary",) * n_axes,
                                vmem_limit_bytes=VMEM_LIMIT_BYTES)


def _fold_kernel(w_in_ref, wf_ref, cc_ref, sc_ref, o_ref):
    hi = lax.Precision.HIGHEST
    for g in range(N_FOURIER_GROUPS):
        wf = wf_ref[g]
        pr = jnp.dot(cc_ref[...], wf, precision=hi, preferred_element_type=F32)
        pi = jnp.dot(sc_ref[...], wf, precision=hi, preferred_element_type=F32)
        wug = w_in_ref[:, g * HEAD_DIM:(g + 1) * HEAD_DIM]
        o_ref[:, g * HEAD_DIM:(g + 1) * HEAD_DIM] = jnp.dot(
            wug, pr, precision=hi, preferred_element_type=F32).astype(BF16)
        o_ref[:, FOURIER_WIDTH + g * HEAD_DIM:FOURIER_WIDTH + (g + 1) * HEAD_DIM] = (-jnp.dot(
            wug, pi, precision=hi, preferred_element_type=F32)).astype(BF16)
    o_ref[:, 2 * FOURIER_WIDTH:] = w_in_ref[:, FOURIER_WIDTH:].astype(BF16)


def _fold_weights(w_in, w_fourier):
    c = np.arange(HEAD_DIM)
    ang = 2.0 * np.pi * np.outer(c, c) / HEAD_DIM
    scale = HEAD_DIM ** -0.5
    cc = jnp.asarray(np.cos(ang) * scale, F32)
    sc = jnp.asarray(np.sin(ang) * scale, F32)
    return pl.pallas_call(
        _fold_kernel,
        out_shape=jax.ShapeDtypeStruct((D_MODEL, PROJ_WIDTH), BF16),
        name="fold",
    )(w_in, w_fourier, cc, sc)


FF_CHUNKS = ((0, 1024), (1024, 2048), (2048, D_FF))


def _swiglu(xn, wg_ref, wu_ref, wd_ref):
    acc = None
    for lo, hi in FF_CHUNKS:
        gate = jnp.dot(xn, wg_ref[:, lo:hi], preferred_element_type=F32)
        up = jnp.dot(xn, wu_ref[:, lo:hi], preferred_element_type=F32)
        act = (gate * jax.nn.sigmoid(gate) * up).astype(BF16)
        part = jnp.dot(act, wd_ref[lo:hi, :], preferred_element_type=F32)
        acc = part if acc is None else acc + part
    return acc


def _ffn_kernel(x_ref, g_ref, wg_ref, wu_ref, wd_ref, *rest):
    n_cast = len(rest) // 2
    o_ref = rest[n_cast]
    for src, dst in zip(rest[:n_cast], rest[n_cast + 1:]):
        dst[...] = src[...].astype(BF16)
    half = x_ref.shape[0] // 2
    for h in range(2):
        rows = slice(h * half, (h + 1) * half)
        x = x_ref[rows, :]
        xn = _rms(x, g_ref[...]).astype(BF16)
        gate = jnp.dot(xn, wg_ref[...], preferred_element_type=F32)
        up = jnp.dot(xn, wu_ref[...], preferred_element_type=F32)
        act = (gate * jax.nn.sigmoid(gate) * up).astype(BF16)
        o_ref[rows, :] = x + 0.5 * jnp.dot(act, wd_ref[...], preferred_element_type=F32)


def _ffn(x, g, wg, wu, wd, cast=()):
    t = x.shape[0]
    tm = TOKEN_TILE
    steps = t // tm
    cast_specs, cast_shapes = [], []
    for m in cast:
        rows, cols = m.shape
        chunks = math.gcd(rows // BF16_SUBLANES, steps)
        per = steps // chunks
        cast_specs.append(pl.BlockSpec((rows // chunks, cols), lambda i, per=per: (i // per, 0)))
        cast_shapes.append(jax.ShapeDtypeStruct(m.shape, BF16))
    out = pl.pallas_call(
        _ffn_kernel,
        grid=(steps,),
        in_specs=[
            pl.BlockSpec((tm, D_MODEL), lambda i: (i, 0)),
            _const_spec((1, D_MODEL)),
            _const_spec((D_MODEL, D_FF)),
            _const_spec((D_MODEL, D_FF)),
            _const_spec((D_FF, D_MODEL)),
            *cast_specs,
        ],
        out_specs=[pl.BlockSpec((tm, D_MODEL), lambda i: (i, 0)), *cast_specs],
        out_shape=[jax.ShapeDtypeStruct((t, D_MODEL), F32), *cast_shapes],
        compiler_params=_params(1),
        name="ffn",
    )(x, g, wg, wu, wd, *cast)
    return out[0], tuple(out[1:])


def _inproj_kernel(x_ref, g_ref, w_ref, ra_ref, rb_ref, ta_ref, tb_ref, d1_ref,
                   t_ref, q_ref, k_ref, v_ref, z_scr, *, n1):
    tmh = INPROJ_ROWS
    n1h = tmh // SUBLANES
    lane = lax.broadcasted_iota(jnp.int32, (tmh, LANES), 1)
    first_half = (lane % HEAD_DIM) < (HEAD_DIM // 2)
    q0 = 2 * FOURIER_WIDTH
    k0 = q0 + ATTN_WIDTH
    for h in range(n1 // n1h):
        s1s = slice(h * n1h, (h + 1) * n1h)
        x = x_ref[s1s].reshape(tmh, D_MODEL)
        hb = _rms(x, g_ref[...]).astype(BF16)
        proj = jnp.dot(hb, w_ref[...], preferred_element_type=F32)
        for c in range(2 * FOURIER_WIDTH // LANES):
            z_scr[c, h * tmh:(h + 1) * tmh, :] = proj[:, c * LANES:(c + 1) * LANES]

        def rows_a(t):
            return jnp.broadcast_to(t[:, None, :], (n1h, SUBLANES, LANES)).reshape(tmh, LANES)

        def rows_b(t):
            return jnp.broadcast_to(t[None, :, :], (n1h, SUBLANES, LANES)).reshape(tmh, LANES)

        ca, sa = rows_a(ra_ref[0, s1s, :]), rows_a(ra_ref[1, s1s, :])
        cos = ca * rows_b(rb_ref[0]) - sa * rows_b(rb_ref[1])
        sin = sa * rows_b(rb_ref[2]) + ca * rows_b(rb_ref[3])

        def rope(xc):
            rot = jnp.where(first_half, pltpu.roll(xc, LANES - HEAD_DIM // 2, 1), pltpu.roll(xc, HEAD_DIM // 2, 1))
            return xc * cos + rot * sin

        for c in range(ATTN_WIDTH // LANES):
            qc = rope(proj[:, q0 + c * LANES:q0 + (c + 1) * LANES]) * (HEAD_DIM ** -0.5)
            q_ref[s1s, :, c * LANES:(c + 1) * LANES] = qc.reshape(n1h, SUBLANES, LANES)
        k_ref[s1s] = rope(proj[:, k0:k0 + KV_WIDTH]).reshape(n1h, SUBLANES, KV_WIDTH)
        v_ref[s1s] = proj[:, k0 + KV_WIDTH:k0 + 2 * KV_WIDTH].reshape(n1h, SUBLANES, KV_WIDTH)

    d1 = d1_ref[...]
    for j in range(SUBLANES):
        rows = pl.ds(j, n1, stride=SUBLANES)
        nch = FOURIER_WIDTH // LANES
        zr = jnp.concatenate([z_scr[c, rows, :] for c in range(nch)], axis=1)
        zi = jnp.concatenate([z_scr[nch + c, rows, :] for c in range(nch)], axis=1)
        st = jnp.concatenate([zr, zi], axis=0).astype(BF16)
        t = jnp.dot(d1, st, preferred_element_type=F32)
        tr, ti = t[:n1], t[n1:]
        twc = ta_ref[0] * tb_ref[0, j] - ta_ref[1] * tb_ref[1, j]
        tws = ta_ref[1] * tb_ref[0, j] + ta_ref[0] * tb_ref[1, j]
        c4 = jnp.concatenate([twc] * (FOURIER_WIDTH // LANES), axis=1)
        s4 = jnp.concatenate([tws] * (FOURIER_WIDTH // LANES), axis=1)
        t_ref[j, :n1, :] = tr * c4 + ti * s4
        t_ref[j, n1:, :] = ti * c4 - tr * s4


def _inproj(x1, g, w, rope_a, rope_b, tw_a, tw_b, d1, *, n1, n2):
    b = x1.shape[0]
    blk = lambda w_: pl.BlockSpec((None, n1, SUBLANES, w_), lambda bi, i: (bi, 0, i, 0))
    tab_b = pl.BlockSpec((4, SUBLANES, LANES), lambda bi, i: (0, i, 0))
    tw_a_spec = pl.BlockSpec((2, None, n1, LANES), lambda bi, i: (0, i, 0, 0))
    return pl.pallas_call(
        functools.partial(_inproj_kernel, n1=n1),
        grid=(b, n2 // SUBLANES),
        in_specs=[blk(D_MODEL), _const_spec((1, D_MODEL)), _const_spec((D_MODEL, PROJ_WIDTH)),
                  _const_spec((2, n1, LANES)), tab_b, tw_a_spec, _const_spec((2, SUBLANES, n1, LANES)),
                  _const_spec((2 * n1, 2 * n1))],
        out_specs=[
            pl.BlockSpec((None, SUBLANES, 2 * n1, FOURIER_WIDTH), lambda bi, i: (bi, i, 0, 0)),
            blk(ATTN_WIDTH), blk(KV_WIDTH), blk(KV_WIDTH),
        ],
        out_shape=[
            jax.ShapeDtypeStruct((b, n2, 2 * n1, FOURIER_WIDTH), F32),
            jax.ShapeDtypeStruct((b, n1, n2, ATTN_WIDTH), F32),
            jax.ShapeDtypeStruct((b, n1, n2, KV_WIDTH), F32),
            jax.ShapeDtypeStruct((b, n1, n2, KV_WIDTH), F32),
        ],
        scratch_shapes=[pltpu.VMEM((2 * FOURIER_WIDTH // LANES, n1 * SUBLANES, LANES), F32)],
        compiler_params=_params(2),
        name="inproj",
    )(x1, g, w, rope_a, rope_b, tw_a, tw_b, d1)


F32_MAX = float(np.finfo(np.float32).max)
GROUP_WIDTH = Q_PER_KV * HEAD_DIM


def _attn_kernel(sink_ref, q_ref, kp_ref, kc_ref, kn_ref, vp_ref, vc_ref, vn_ref, o_ref, *, nb, nblk_seq):
    i = pl.program_id(0)
    rows = (nb + 2) * BLOCK
    kfull = jnp.concatenate([kp_ref[...], kc_ref[...], kn_ref[...]], axis=0)
    vfull = jnp.concatenate([vp_ref[...], vc_ref[...], vn_ref[...]], axis=0)
    lo = lax.broadcasted_iota(jnp.int32, (rows, LANES), 1) < HEAD_DIM

    def dup(x):
        r = pltpu.roll(x, HEAD_DIM, 1)
        return jnp.where(lo, x, r), jnp.where(lo, r, x)

    kt = jnp.transpose(kfull).astype(BF16)
    zero_kt = jnp.zeros((HEAD_DIM, 3 * BLOCK), BF16)
    vv = dup(vfull)
    wkeys = 3 * BLOCK
    grp = lax.broadcasted_iota(jnp.int32, (wkeys, GROUP_WIDTH), 1) // HEAD_DIM
    ogrp = lax.broadcasted_iota(jnp.int32, (BLOCK, GROUP_WIDTH), 1) // HEAD_DIM
    kidx = lax.broadcasted_iota(jnp.int32, (BLOCK, BLOCK), 1)
    qidx = lax.broadcasted_iota(jnp.int32, (BLOCK, BLOCK), 0)
    cap_first = jnp.where(kidx >= qidx, F32_MAX, NEG_INF)
    cap_last = jnp.where(kidx <= qidx, F32_MAX, NEG_INF)

    def blockdiag(x):
        wide = jnp.concatenate([x, x], axis=1)
        return jnp.concatenate([jnp.where(grp == g, wide, 0.0) for g in range(Q_PER_KV)], axis=0).astype(BF16)

    def blockdiag_t(xt):
        return jnp.concatenate(
            [jnp.concatenate([xt if gg == g else zero_kt for gg in range(Q_PER_KV)], axis=1)
             for g in range(Q_PER_KV)], axis=0)

    caps = []
    for n in range(nb):
        pb = (i * nb + n) % nblk_seq
        caps.append((jnp.minimum(cap_first, jnp.where(pb > 0, F32_MAX, NEG_INF)),
                     jnp.minimum(cap_last, jnp.where(pb < nblk_seq - 1, F32_MAX, NEG_INF))))

    def scores(n, kh):
        kbd_t = blockdiag_t(kt[kh * HEAD_DIM:(kh + 1) * HEAD_DIM, n * BLOCK:n * BLOCK + wkeys])
        c0 = kh * GROUP_WIDTH
        qh = q_ref[n * BLOCK:(n + 1) * BLOCK, c0:c0 + GROUP_WIDTH].astype(BF16)
        return jnp.dot(qh, kbd_t, preferred_element_type=F32)

    def softmax(n, kh, s_all):
        capf, capl = caps[n]
        tiles, maxes = [], []
        for g in range(Q_PER_KV):
            s0 = jnp.minimum(s_all[:, g * wkeys:g * wkeys + BLOCK], capf)
            s1 = s_all[:, g * wkeys + BLOCK:g * wkeys + 2 * BLOCK]
            s2 = jnp.minimum(s_all[:, g * wkeys + 2 * BLOCK:(g + 1) * wkeys], capl)
            tiles.append((s0, s1, s2))
            maxes.append(jnp.max(jnp.maximum(jnp.maximum(s0, s1), s2), axis=-1, keepdims=True))
        ps, dens = [], []
        for g in range(Q_PER_KV):
            p = [jnp.exp((t - maxes[g]).astype(BF16)) for t in tiles[g]]
            psum = (p[0] + p[1] + p[2]).astype(F32)
            dens.append(jnp.sum(psum, axis=-1, keepdims=True) + jnp.exp(sink_ref[kh * Q_PER_KV + g] - maxes[g]))
            ps += p
        scale = 1.0 / dens[Q_PER_KV - 1]
        for g in range(Q_PER_KV - 2, -1, -1):
            scale = jnp.where(ogrp == g, 1.0 / dens[g], scale)
        return jnp.concatenate(ps, axis=1), scale

    def output(n, kh, p_all, scale):
        vbd = blockdiag(vv[kh][n * BLOCK:n * BLOCK + wkeys])
        c0 = kh * GROUP_WIDTH
        o_ref[n * BLOCK:(n + 1) * BLOCK, c0:c0 + GROUP_WIDTH] = jnp.dot(
            p_all, vbd, preferred_element_type=F32) * scale

    units = [(n, kh) for kh in range(N_KV_HEADS) for n in range(nb)]
    s_prev = None
    for k in range(len(units) + 1):
        s_cur = scores(*units[k]) if k < len(units) else None
        if s_prev is not None:
            output(*units[k - 1], *softmax(*units[k - 1], s_prev))
        s_prev = s_cur


def _attention(q, k, v, sink, *, seq, nb=16):
    t = q.shape[0]
    nblk = t // BLOCK
    tq = nb * BLOCK
    cur = lambda w_: pl.BlockSpec((tq, w_), lambda i: (i, 0))
    prev = pl.BlockSpec((BLOCK, KV_WIDTH), lambda i: (jnp.maximum(i * nb - 1, 0), 0))
    nxt = pl.BlockSpec((BLOCK, KV_WIDTH), lambda i: (jnp.minimum(i * nb + nb, nblk - 1), 0))
    return pl.pallas_call(
        functools.partial(_attn_kernel, nb=nb, nblk_seq=seq // BLOCK),
        grid=(t // tq,),
        in_specs=[pl.BlockSpec(memory_space=pltpu.SMEM), cur(ATTN_WIDTH),
                  prev, cur(KV_WIDTH), nxt, prev, cur(KV_WIDTH), nxt],
        out_specs=cur(ATTN_WIDTH),
        out_shape=jax.ShapeDtypeStruct((t, ATTN_WIDTH), F32),
        compiler_params=_params(1),
        name="attention",
    )(sink, q, k, k, k, v, v, v)


def _mixffn_kernel(tr_ref, ti_ref, x_ref, ya_ref, d2_ref, wo_ref, g2_ref, wg_ref, wu_ref, wd_ref, gf_ref,
                   o_ref, tr_scr, ti_scr, yf_scr, *, n2, n2h):
    tm = n2h * SUBLANES
    nch = FOURIER_WIDTH // LANES

    @pl.when(pl.program_id(2) == 0)
    def _():
        for c in range(nch):
            tr_scr[c] = tr_ref[:, :, c * LANES:(c + 1) * LANES].reshape(n2 * SUBLANES, LANES)
            ti_scr[c] = ti_ref[:, :, c * LANES:(c + 1) * LANES].reshape(n2 * SUBLANES, LANES)

    d2 = d2_ref[...]
    for j in range(SUBLANES):
        rows = pl.ds(j, n2, stride=SUBLANES)
        tr = jnp.concatenate([tr_scr[c, rows, :] for c in range(nch)], axis=1)
        ti = jnp.concatenate([ti_scr[c, rows, :] for c in range(nch)], axis=1)
        st = jnp.concatenate([tr, ti], axis=0).astype(BF16)
        yfj = jnp.dot(d2, st, preferred_element_type=F32)
        for c in range(nch):
            yf_scr[c, pl.ds(j, n2h, stride=SUBLANES), :] = yfj[:, c * LANES:(c + 1) * LANES]
    yf = jnp.concatenate([yf_scr[c] for c in range(nch)], axis=1).astype(BF16)
    ya = ya_ref[...].reshape(tm, ATTN_WIDTH).astype(BF16)
    x2 = (x_ref[...].reshape(tm, D_MODEL)
          + jnp.dot(yf, wo_ref[:FOURIER_WIDTH, :], preferred_element_type=F32)
          + jnp.dot(ya, wo_ref[FOURIER_WIDTH:, :], preferred_element_type=F32))

    half = tm // 2
    for h in range(2):
        xh = x2[h * half:(h + 1) * half]
        xn = _rms(xh, g2_ref[...]).astype(BF16)
        y = _rms(xh + 0.5 * _swiglu(xn, wg_ref, wu_ref, wd_ref), gf_ref[...])
        o_ref[h * (n2h // 2):(h + 1) * (n2h // 2)] = y.reshape(n2h // 2, SUBLANES, D_MODEL)


def _mixffn(t_arr, x1, ya, d2, wo, g2, wg, wu, wd, gf, *, n1, n2):
    b = x1.shape[0]
    halves = (n2 * SUBLANES) // TOKEN_TILE
    n2h = n2 // halves
    t_re = pl.BlockSpec((None, n2, SUBLANES, FOURIER_WIDTH), lambda bi, i, h: (bi, 0, i, 0))
    t_im = pl.BlockSpec((None, n2, SUBLANES, FOURIER_WIDTH), lambda bi, i, h: (bi, 0, n1 // SUBLANES + i, 0))
    blk = lambda w_: pl.BlockSpec((None, n2h, SUBLANES, w_), lambda bi, i, h: (bi, h, i, 0))
    return pl.pallas_call(
        functools.partial(_mixffn_kernel, n2=n2, n2h=n2h),
        grid=(b, n1 // SUBLANES, halves),
        in_specs=[t_re, t_im, blk(D_MODEL), blk(ATTN_WIDTH),
                  pl.BlockSpec((n2h, 2 * n2), lambda bi, i, h: (h, 0)),
                  _const_spec((2 * FOURIER_WIDTH, D_MODEL)), _const_spec((1, D_MODEL)),
                  _const_spec((D_MODEL, D_FF)), _const_spec((D_MODEL, D_FF)), _const_spec((D_FF, D_MODEL)),
                  _const_spec((1, D_MODEL))],
        out_specs=blk(D_MODEL),
        out_shape=jax.ShapeDtypeStruct((b, n2, n1, D_MODEL), F32),
        scratch_shapes=[pltpu.VMEM((FOURIER_WIDTH // LANES, n2 * SUBLANES, LANES), F32)] * 2
        + [pltpu.VMEM((FOURIER_WIDTH // LANES, n2h * SUBLANES, LANES), F32)],
        compiler_params=_params(3),
        name="mixffn",
    )(t_arr, t_arr, x1, ya, d2, wo, g2, wg, wu, wd, gf)


def _dft_tables(n1, n2):
    s = n1 * n2
    a1 = 2.0 * np.pi * np.outer(np.arange(n1), np.arange(n1)) / n1
    c1, s1 = np.cos(a1) / np.sqrt(n1), np.sin(a1) / np.sqrt(n1)
    d1 = np.block([[c1, s1], [-s1, c1]])
    a2 = 2.0 * np.pi * np.outer(np.arange(n2), np.arange(n2)) / n2
    d2 = np.concatenate([np.cos(a2), np.sin(a2)], axis=1) / np.sqrt(n2)
    k1 = jnp.arange(n1, dtype=jnp.int32)[None, :]

    def table(mult, count):
        m = (jnp.arange(count, dtype=jnp.int32)[:, None] * mult * k1) % s
        ang = m.astype(F32) * (2.0 * np.pi / s)
        return jnp.broadcast_to(jnp.stack([jnp.cos(ang), jnp.sin(ang)])[..., None], (2, count, n1, LANES))

    tw_a, tw_b = table(SUBLANES, n2 // SUBLANES), table(1, SUBLANES)
    return jnp.asarray(d1, F32).astype(BF16), jnp.asarray(d2, F32).astype(BF16), tw_a, tw_b


def _rope_tables(n1, n2):
    half = HEAD_DIM // 2
    lane = np.arange(LANES)
    inv_freq = ROPE_THETA ** (-(lane % half) / half)
    ang_a = (np.arange(n1) * n2)[:, None] * inv_freq[None, :]
    ang_b = np.arange(n2)[:, None] * inv_freq[None, :]
    sign = np.where((lane % HEAD_DIM) < half, -1.0, 1.0)[None, :]
    rope_a = np.stack([np.cos(ang_a), np.sin(ang_a)])
    rope_b = np.stack([np.cos(ang_b), np.sin(ang_b), sign * np.cos(ang_b), sign * np.sin(ang_b)])
    return jnp.asarray(rope_a, F32), jnp.asarray(rope_b, F32)


def _trunk(x, w, *, n1, n2, w2_f32=None):
    b, seq, _ = x.shape
    t = b * seq
    d1, d2, tw_a, tw_b = _dft_tables(n1, n2)
    rope_a, rope_b = _rope_tables(n1, n2)
    x1, w2 = _ffn(x.reshape(t, D_MODEL), w["g_ffn1"], w["w1_gate"], w["w1_up"], w["w1_down"],
                  cast=w2_f32 or ())
    if w2_f32 is None:
        w2 = (w["w2_gate"], w["w2_up"], w["w2_down"])
    t_arr, q, k, v = _inproj(x1.reshape(b, n1, n2, D_MODEL), w["g_mix"], w["w_proj"], rope_a, rope_b,
                             tw_a, tw_b, d1, n1=n1, n2=n2)
    ya = _attention(q.reshape(t, ATTN_WIDTH), k.reshape(t, KV_WIDTH), v.reshape(t, KV_WIDTH), w["attn_sink"],
                    seq=seq)
    y = _mixffn(t_arr, x1.reshape(b, n2, n1, D_MODEL), ya.reshape(b, n2, n1, ATTN_WIDTH), d2, w["w_out"],
                w["g_ffn2"], *w2, w["g_final"], n1=n1, n2=n2)
    return y.reshape(b, seq, D_MODEL), w2


def kernel(x_prompt, x_sample, g_ffn1, w1_gate, w1_up, w1_down, g_mix, w_in, w_fourier, attn_sink, w_out,
           g_ffn2, w2_gate, w2_up, w2_down, g_final):
    assert g_ffn1.shape[0] == 1, "single-layer trunk"
    w = {
        "g_ffn1": g_ffn1[0][None, :], "g_mix": g_mix[0][None, :], "g_ffn2": g_ffn2[0][None, :],
        "g_final": g_final[None, :],
        "w1_gate": w1_gate[0].astype(BF16), "w1_up": w1_up[0].astype(BF16), "w1_down": w1_down[0].astype(BF16),
        "w_proj": _fold_weights(w_in[0], w_fourier[0]),
        "w_out": w_out[0].astype(BF16),
        "attn_sink": attn_sink[0],
    }
    y_prompt, w2 = _trunk(x_prompt, w, n1=128, n2=128, w2_f32=(w2_gate[0], w2_up[0], w2_down[0]))
    w["w2_gate"], w["w2_up"], w["w2_down"] = w2
    y_sample, _ = _trunk(x_sample, w, n1=64, n2=64)
    return (y_prompt, y_sample)
```

```python
import functools
import math

import numpy as np
import jax
import jax.numpy as jnp
from jax import lax
from jax.experimental import pallas as pl
from jax.experimental.pallas import tpu as pltpu

D_MODEL = 1024
HEAD_DIM = 64
N_FOURIER_GROUPS = 8
FOURIER_WIDTH = N_FOURIER_GROUPS * HEAD_DIM
N_Q_HEADS = 8
N_KV_HEADS = 2
Q_PER_KV = N_Q_HEADS // N_KV_HEADS
ATTN_WIDTH = N_Q_HEADS * HEAD_DIM
KV_WIDTH = N_KV_HEADS * HEAD_DIM
D_FF = 2816
WINDOW = 128
BLOCK = 128
ROPE_THETA = 10000.0
RMS_EPS = 1e-6
NEG_INF = -1e30

SUBLANES = 8
BF16_SUBLANES = 16
LANES = 128
PROJ_WIDTH = 2 * FOURIER_WIDTH + ATTN_WIDTH + 2 * KV_WIDTH
VMEM_LIMIT_BYTES = 56 * 1024 * 1024
TOKEN_TILE = 512
INPROJ_ROWS = 256

F32 = jnp.float32
BF16 = jnp.bfloat16


def _rms(x, g):
    return x * lax.rsqrt(jnp.mean(x * x, axis=-1, keepdims=True) + RMS_EPS) * g


def _const_spec(shape):
    zeros = (0,) * len(shape)
    return pl.BlockSpec(shape, lambda *_: zeros, pipeline_mode=pl.Buffered(1))


def _params(n_axes):
    return pltpu.CompilerParams(dimension_semantics=("arbitrary",) * n_axes,
                                vmem_limit_bytes=VMEM_LIMIT_BYTES)


def _fold_kernel(w_in_ref, wf_ref, cc_ref, sc_ref, o_ref):
    hi = lax.Precision.HIGHEST
    for g in range(N_FOURIER_GROUPS):
        wf = wf_ref[g]
        pr = jnp.dot(cc_ref[...], wf, precision=hi, preferred_element_type=F32)
        pi = jnp.dot(sc_ref[...], wf, precision=hi, preferred_element_type=F32)
        wug = w_in_ref[:, g * HEAD_DIM:(g + 1) * HEAD_DIM]
        o_ref[:, g * HEAD_DIM:(g + 1) * HEAD_DIM] = jnp.dot(
            wug, pr, precision=hi, preferred_element_type=F32).astype(BF16)
        o_ref[:, FOURIER_WIDTH + g * HEAD_DIM:FOURIER_WIDTH + (g + 1) * HEAD_DIM] = (-jnp.dot(
            wug, pi, precision=hi, preferred_element_type=F32)).astype(BF16)
    o_ref[:, 2 * FOURIER_WIDTH:] = w_in_ref[:, FOURIER_WIDTH:].astype(BF16)


def _fold_weights(w_in, w_fourier):
    c = np.arange(HEAD_DIM)
    ang = 2.0 * np.pi * np.outer(c, c) / HEAD_DIM
    scale = HEAD_DIM ** -0.5
    cc = jnp.asarray(np.cos(ang) * scale, F32)
    sc = jnp.asarray(np.sin(ang) * scale, F32)
    return pl.pallas_call(
        _fold_kernel,
        out_shape=jax.ShapeDtypeStruct((D_MODEL, PROJ_WIDTH), BF16),
        name="fold",
    )(w_in, w_fourier, cc, sc)


FF_CHUNKS = ((0, 1024), (1024, 2048), (2048, D_FF))


def _swiglu(xn, wg_ref, wu_ref, wd_ref):
    acc = None
    for lo, hi in FF_CHUNKS:
        gate = jnp.dot(xn, wg_ref[:, lo:hi], preferred_element_type=F32)
        up = jnp.dot(xn, wu_ref[:, lo:hi], preferred_element_type=F32)
        act = (gate * jax.nn.sigmoid(gate) * up).astype(BF16)
        part = jnp.dot(act, wd_ref[lo:hi, :], preferred_element_type=F32)
        acc = part if acc is None else acc + part
    return acc


def _ffn_kernel(x_ref, g_ref, wg_ref, wu_ref, wd_ref, *rest):
    n_cast = len(rest) // 2
    o_ref = rest[n_cast]
    for src, dst in zip(rest[:n_cast], rest[n_cast + 1:]):
        dst[...] = src[...].astype(BF16)
    half = x_ref.shape[0] // 2
    for h in range(2):
        rows = slice(h * half, (h + 1) * half)
        x = x_ref[rows, :]
        xn = _rms(x, g_ref[...]).astype(BF16)
        gate = jnp.dot(xn, wg_ref[...], preferred_element_type=F32)
        up = jnp.dot(xn, wu_ref[...], preferred_element_type=F32)
        act = (gate * jax.nn.sigmoid(gate) * up).astype(BF16)
        o_ref[rows, :] = x + 0.5 * jnp.dot(act, wd_ref[...], preferred_element_type=F32)


def _ffn(x, g, wg, wu, wd, cast=()):
    t = x.shape[0]
    tm = TOKEN_TILE
    steps = t // tm
    cast_specs, cast_shapes = [], []
    for m in cast:
        rows, cols = m.shape
        chunks = math.gcd(rows // BF16_SUBLANES, steps)
        per = steps // chunks
        cast_specs.append(pl.BlockSpec((rows // chunks, cols), lambda i, per=per: (i // per, 0)))
        cast_shapes.append(jax.ShapeDtypeStruct(m.shape, BF16))
    out = pl.pallas_call(
        _ffn_kernel,
        grid=(steps,),
        in_specs=[
            pl.BlockSpec((tm, D_MODEL), lambda i: (i, 0)),
            _const_spec((1, D_MODEL)),
            _const_spec((D_MODEL, D_FF)),
            _const_spec((D_MODEL, D_FF)),
            _const_spec((D_FF, D_MODEL)),
            *cast_specs,
        ],
        out_specs=[pl.BlockSpec((tm, D_MODEL), lambda i: (i, 0)), *cast_specs],
        out_shape=[jax.ShapeDtypeStruct((t, D_MODEL), F32), *cast_shapes],
        compiler_params=_params(1),
        name="ffn",
    )(x, g, wg, wu, wd, *cast)
    return out[0], tuple(out[1:])


def _inproj_kernel(x_ref, g_ref, w_ref, ra_ref, rb_ref, ta_ref, tb_ref, d1_ref,
                   t_ref, q_ref, k_ref, v_ref, z_scr, *, n1):
    tmh = INPROJ_ROWS
    n1h = tmh // SUBLANES
    lane = lax.broadcasted_iota(jnp.int32, (tmh, LANES), 1)
    first_half = (lane % HEAD_DIM) < (HEAD_DIM // 2)
    q0 = 2 * FOURIER_WIDTH
    k0 = q0 + ATTN_WIDTH
    for h in range(n1 // n1h):
        s1s = slice(h * n1h, (h + 1) * n1h)
        x = x_ref[s1s].reshape(tmh, D_MODEL)
        hb = _rms(x, g_ref[...]).astype(BF16)
        proj = jnp.dot(hb, w_ref[...], preferred_element_type=F32)
        for c in range(2 * FOURIER_WIDTH // LANES):
            z_scr[c, h * tmh:(h + 1) * tmh, :] = proj[:, c * LANES:(c + 1) * LANES]

        def rows_a(t):
            return jnp.broadcast_to(t[:, None, :], (n1h, SUBLANES, LANES)).reshape(tmh, LANES)

        def rows_b(t):
            return jnp.broadcast_to(t[None, :, :], (n1h, SUBLANES, LANES)).reshape(tmh, LANES)

        ca, sa = rows_a(ra_ref[0, s1s, :]), rows_a(ra_ref[1, s1s, :])
        cos = ca * rows_b(rb_ref[0]) - sa * rows_b(rb_ref[1])
        sin = sa * rows_b(rb_ref[2]) + ca * rows_b(rb_ref[3])

        def rope(xc):
            rot = jnp.where(first_half, pltpu.roll(xc, LANES - HEAD_DIM // 2, 1), pltpu.roll(xc, HEAD_DIM // 2, 1))
            return xc * cos + rot * sin

        for c in range(ATTN_WIDTH // LANES):
            qc = rope(proj[:, q0 + c * LANES:q0 + (c + 1) * LANES]) * (HEAD_DIM ** -0.5)
            q_ref[s1s, :, c * LANES:(c + 1) * LANES] = qc.reshape(n1h, SUBLANES, LANES)
        k_ref[s1s] = rope(proj[:, k0:k0 + KV_WIDTH]).reshape(n1h, SUBLANES, KV_WIDTH)
        v_ref[s1s] = proj[:, k0 + KV_WIDTH:k0 + 2 * KV_WIDTH].reshape(n1h, SUBLANES, KV_WIDTH)

    d1 = d1_ref[...]
    for j in range(SUBLANES):
        rows = pl.ds(j, n1, stride=SUBLANES)
        nch = FOURIER_WIDTH // LANES
        zr = jnp.concatenate([z_scr[c, rows, :] for c in range(nch)], axis=1)
        zi = jnp.concatenate([z_scr[nch + c, rows, :] for c in range(nch)], axis=1)
        st = jnp.concatenate([zr, zi], axis=0).astype(BF16)
        t = jnp.dot(d1, st, preferred_element_type=F32)
        tr, ti = t[:n1], t[n1:]
        twc = ta_ref[0] * tb_ref[0, j] - ta_ref[1] * tb_ref[1, j]
        tws = ta_ref[1] * tb_ref[0, j] + ta_ref[0] * tb_ref[1, j]
        c4 = jnp.concatenate([twc] * (FOURIER_WIDTH // LANES), axis=1)
        s4 = jnp.concatenate([tws] * (FOURIER_WIDTH // LANES), axis=1)
        t_ref[j, :n1, :] = tr * c4 + ti * s4
        t_ref[j, n1:, :] = ti * c4 - tr * s4


def _inproj(x1, g, w, rope_a, rope_b, tw_a, tw_b, d1, *, n1, n2):
    b = x1.shape[0]
    blk = lambda w_: pl.BlockSpec((None, n1, SUBLANES, w_), lambda bi, i: (bi, 0, i, 0))
    tab_b = pl.BlockSpec((4, SUBLANES, LANES), lambda bi, i: (0, i, 0))
    tw_a_spec = pl.BlockSpec((2, None, n1, LANES), lambda bi, i: (0, i, 0, 0))
    return pl.pallas_call(
        functools.partial(_inproj_kernel, n1=n1),
        grid=(b, n2 // SUBLANES),
        in_specs=[blk(D_MODEL), _const_spec((1, D_MODEL)), _const_spec((D_MODEL, PROJ_WIDTH)),
                  _const_spec((2, n1, LANES)), tab_b, tw_a_spec, _const_spec((2, SUBLANES, n1, LANES)),
                  _const_spec((2 * n1, 2 * n1))],
        out_specs=[
            pl.BlockSpec((None, SUBLANES, 2 * n1, FOURIER_WIDTH), lambda bi, i: (bi, i, 0, 0)),
            blk(ATTN_WIDTH), blk(KV_WIDTH), blk(KV_WIDTH),
        ],
        out_shape=[
            jax.ShapeDtypeStruct((b, n2, 2 * n1, FOURIER_WIDTH), F32),
            jax.ShapeDtypeStruct((b, n1, n2, ATTN_WIDTH), F32),
            jax.ShapeDtypeStruct((b, n1, n2, KV_WIDTH), F32),
            jax.ShapeDtypeStruct((b, n1, n2, KV_WIDTH), F32),
        ],
        scratch_shapes=[pltpu.VMEM((2 * FOURIER_WIDTH // LANES, n1 * SUBLANES, LANES), F32)],
        compiler_params=_params(2),
        name="inproj",
    )(x1, g, w, rope_a, rope_b, tw_a, tw_b, d1)


F32_MAX = float(np.finfo(np.float32).max)
GROUP_WIDTH = Q_PER_KV * HEAD_DIM


def _attn_kernel(sink_ref, q_ref, kp_ref, kc_ref, kn_ref, vp_ref, vc_ref, vn_ref, o_ref, *, nb, nblk_seq):
    i = pl.program_id(0)
    rows = (nb + 2) * BLOCK
    kfull = jnp.concatenate([kp_ref[...], kc_ref[...], kn_ref[...]], axis=0)
    vfull = jnp.concatenate([vp_ref[...], vc_ref[...], vn_ref[...]], axis=0)
    lo = lax.broadcasted_iota(jnp.int32, (rows, LANES), 1) < HEAD_DIM

    def dup(x):
        r = pltpu.roll(x, HEAD_DIM, 1)
        return jnp.where(lo, x, r), jnp.where(lo, r, x)

    kt = jnp.transpose(kfull).astype(BF16)
    zero_kt = jnp.zeros((HEAD_DIM, 3 * BLOCK), BF16)
    vv = dup(vfull)
    wkeys = 3 * BLOCK
    grp = lax.broadcasted_iota(jnp.int32, (wkeys, GROUP_WIDTH), 1) // HEAD_DIM
    ogrp = lax.broadcasted_iota(jnp.int32, (BLOCK, GROUP_WIDTH), 1) // HEAD_DIM
    kidx = lax.broadcasted_iota(jnp.int32, (BLOCK, BLOCK), 1)
    qidx = lax.broadcasted_iota(jnp.int32, (BLOCK, BLOCK), 0)
    cap_first = jnp.where(kidx >= qidx, F32_MAX, NEG_INF)
    cap_last = jnp.where(kidx <= qidx, F32_MAX, NEG_INF)

    def blockdiag(x):
        wide = jnp.concatenate([x, x], axis=1)
        return jnp.concatenate([jnp.where(grp == g, wide, 0.0) for g in range(Q_PER_KV)], axis=0).astype(BF16)

    def blockdiag_t(xt):
        return jnp.concatenate(
            [jnp.concatenate([xt if gg == g else zero_kt for gg in range(Q_PER_KV)], axis=1)
             for g in range(Q_PER_KV)], axis=0)

    caps = []
    for n in range(nb):
        pb = (i * nb + n) % nblk_seq
        caps.append((jnp.minimum(cap_first, jnp.where(pb > 0, F32_MAX, NEG_INF)),
                     jnp.minimum(cap_last, jnp.where(pb < nblk_seq - 1, F32_MAX, NEG_INF))))

    def scores(n, kh):
        kbd_t = blockdiag_t(kt[kh * HEAD_DIM:(kh + 1) * HEAD_DIM, n * BLOCK:n * BLOCK + wkeys])
        c0 = kh * GROUP_WIDTH
        qh = q_ref[n * BLOCK:(n + 1) * BLOCK, c0:c0 + GROUP_WIDTH].astype(BF16)
        return jnp.dot(qh, kbd_t, preferred_element_type=F32)

    def softmax(n, kh, s_all):
        capf, capl = caps[n]
        tiles, maxes = [], []
        for g in range(Q_PER_KV):
            s0 = jnp.minimum(s_all[:, g * wkeys:g * wkeys + BLOCK], capf)
            s1 = s_all[:, g * wkeys + BLOCK:g * wkeys + 2 * BLOCK]
            s2 = jnp.minimum(s_all[:, g * wkeys + 2 * BLOCK:(g + 1) * wkeys], capl)
            tiles.append((s0, s1, s2))
            maxes.append(jnp.max(jnp.maximum(jnp.maximum(s0, s1), s2), axis=-1, keepdims=True))
        ps, dens = [], []
        for g in range(Q_PER_KV):
            p = [jnp.exp((t - maxes[g]).astype(BF16)) for t in tiles[g]]
            psum = (p[0] + p[1] + p[2]).astype(F32)
            dens.append(jnp.sum(psum, axis=-1, keepdims=True) + jnp.exp(sink_ref[kh * Q_PER_KV + g] - maxes[g]))
            ps += p
        scale = 1.0 / dens[Q_PER_KV - 1]
        for g in range(Q_PER_KV - 2, -1, -1):
            scale = jnp.where(ogrp == g, 1.0 / dens[g], scale)
        return jnp.concatenate(ps, axis=1), scale

    def output(n, kh, p_all, scale):
        vbd = blockdiag(vv[kh][n * BLOCK:n * BLOCK + wkeys])
        c0 = kh * GROUP_WIDTH
        o_ref[n * BLOCK:(n + 1) * BLOCK, c0:c0 + GROUP_WIDTH] = jnp.dot(
            p_all, vbd, preferred_element_type=F32) * scale

    units = [(n, kh) for kh in range(N_KV_HEADS) for n in range(nb)]
    s_prev = None
    for k in range(len(units) + 1):
        s_cur = scores(*units[k]) if k < len(units) else None
        if s_prev is not None:
            output(*units[k - 1], *softmax(*units[k - 1], s_prev))
        s_prev = s_cur


def _attention(q, k, v, sink, *, seq, nb=16):
    t = q.shape[0]
    nblk = t // BLOCK
    tq = nb * BLOCK
    cur = lambda w_: pl.BlockSpec((tq, w_), lambda i: (i, 0))
    prev = pl.BlockSpec((BLOCK, KV_WIDTH), lambda i: (jnp.maximum(i * nb - 1, 0), 0))
    nxt = pl.BlockSpec((BLOCK, KV_WIDTH), lambda i: (jnp.minimum(i * nb + nb, nblk - 1), 0))
    return pl.pallas_call(
        functools.partial(_attn_kernel, nb=nb, nblk_seq=seq // BLOCK),
        grid=(t // tq,),
        in_specs=[pl.BlockSpec(memory_space=pltpu.SMEM), cur(ATTN_WIDTH),
                  prev, cur(KV_WIDTH), nxt, prev, cur(KV_WIDTH), nxt],
        out_specs=cur(ATTN_WIDTH),
        out_shape=jax.ShapeDtypeStruct((t, ATTN_WIDTH), F32),
        compiler_params=_params(1),
        name="attention",
    )(sink, q, k, k, k, v, v, v)


def _mixffn_kernel(tr_ref, ti_ref, x_ref, ya_ref, d2_ref, wo_ref, g2_ref, wg_ref, wu_ref, wd_ref, gf_ref,
                   o_ref, tr_scr, ti_scr, yf_scr, *, n2, n2h):
    tm = n2h * SUBLANES
    nch = FOURIER_WIDTH // LANES

    @pl.when(pl.program_id(2) == 0)
    def _():
        for c in range(nch):
            tr_scr[c] = tr_ref[:, :, c * LANES:(c + 1) * LANES].reshape(n2 * SUBLANES, LANES)
            ti_scr[c] = ti_ref[:, :, c * LANES:(c + 1) * LANES].reshape(n2 * SUBLANES, LANES)

    d2 = d2_ref[...]
    for j in range(SUBLANES):
        rows = pl.ds(j, n2, stride=SUBLANES)
        tr = jnp.concatenate([tr_scr[c, rows, :] for c in range(nch)], axis=1)
        ti = jnp.concatenate([ti_scr[c, rows, :] for c in range(nch)], axis=1)
        st = jnp.concatenate([tr, ti], axis=0).astype(BF16)
        yfj = jnp.dot(d2, st, preferred_element_type=F32)
        for c in range(nch):
            yf_scr[c, pl.ds(j, n2h, stride=SUBLANES), :] = yfj[:, c * LANES:(c + 1) * LANES]
    yf = jnp.concatenate([yf_scr[c] for c in range(nch)], axis=1).astype(BF16)
    ya = ya_ref[...].reshape(tm, ATTN_WIDTH).astype(BF16)
    x2 = (x_ref[...].reshape(tm, D_MODEL)
          + jnp.dot(yf, wo_ref[:FOURIER_WIDTH, :], preferred_element_type=F32)
          + jnp.dot(ya, wo_ref[FOURIER_WIDTH:, :], preferred_element_type=F32))

    half = tm // 2
    for h in range(2):
        xh = x2[h * half:(h + 1) * half]
        xn = _rms(xh, g2_ref[...]).astype(BF16)
        y = _rms(xh + 0.5 * _swiglu(xn, wg_ref, wu_ref, wd_ref), gf_ref[...])
        o_ref[h * (n2h // 2):(h + 1) * (n2h // 2)] = y.reshape(n2h // 2, SUBLANES, D_MODEL)


def _mixffn(t_arr, x1, ya, d2, wo, g2, wg, wu, wd, gf, *, n1, n2):
    b = x1.shape[0]
    halves = (n2 * SUBLANES) // TOKEN_TILE
    n2h = n2 // halves
    t_re = pl.BlockSpec((None, n2, SUBLANES, FOURIER_WIDTH), lambda bi, i, h: (bi, 0, i, 0))
    t_im = pl.BlockSpec((None, n2, SUBLANES, FOURIER_WIDTH), lambda bi, i, h: (bi, 0, n1 // SUBLANES + i, 0))
    blk = lambda w_: pl.BlockSpec((None, n2h, SUBLANES, w_), lambda bi, i, h: (bi, h, i, 0))
    return pl.pallas_call(
        functools.partial(_mixffn_kernel, n2=n2, n2h=n2h),
        grid=(b, n1 // SUBLANES, halves),
        in_specs=[t_re, t_im, blk(D_MODEL), blk(ATTN_WIDTH),
                  pl.BlockSpec((n2h, 2 * n2), lambda bi, i, h: (h, 0)),
                  _const_spec((2 * FOURIER_WIDTH, D_MODEL)), _const_spec((1, D_MODEL)),
                  _const_spec((D_MODEL, D_FF)), _const_spec((D_MODEL, D_FF)), _const_spec((D_FF, D_MODEL)),
                  _const_spec((1, D_MODEL))],
        out_specs=blk(D_MODEL),
        out_shape=jax.ShapeDtypeStruct((b, n2, n1, D_MODEL), F32),
        scratch_shapes=[pltpu.VMEM((FOURIER_WIDTH // LANES, n2 * SUBLANES, LANES), F32)] * 2
        + [pltpu.VMEM((FOURIER_WIDTH // LANES, n2h * SUBLANES, LANES), F32)],
        compiler_params=_params(3),
        name="mixffn",
    )(t_arr, t_arr, x1, ya, d2, wo, g2, wg, wu, wd, gf)


def _dft_tables(n1, n2):
    s = n1 * n2
    a1 = 2.0 * np.pi * np.outer(np.arange(n1), np.arange(n1)) / n1
    c1, s1 = np.cos(a1) / np.sqrt(n1), np.sin(a1) / np.sqrt(n1)
    d1 = np.block([[c1, s1], [-s1, c1]])
    a2 = 2.0 * np.pi * np.outer(np.arange(n2), np.arange(n2)) / n2
    d2 = np.concatenate([np.cos(a2), np.sin(a2)], axis=1) / np.sqrt(n2)
    k1 = jnp.arange(n1, dtype=jnp.int32)[None, :]

    def table(mult, count):
        m = (jnp.arange(count, dtype=jnp.int32)[:, None] * mult * k1) % s
        ang = m.astype(F32) * (2.0 * np.pi / s)
        return jnp.broadcast_to(jnp.stack([jnp.cos(ang), jnp.sin(ang)])[..., None], (2, count, n1, LANES))

    tw_a, tw_b = table(SUBLANES, n2 // SUBLANES), table(1, SUBLANES)
    return jnp.asarray(d1, F32).astype(BF16), jnp.asarray(d2, F32).astype(BF16), tw_a, tw_b


def _rope_tables(n1, n2):
    half = HEAD_DIM // 2
    lane = np.arange(LANES)
    inv_freq = ROPE_THETA ** (-(lane % half) / half)
    ang_a = (np.arange(n1) * n2)[:, None] * inv_freq[None, :]
    ang_b = np.arange(n2)[:, None] * inv_freq[None, :]
    sign = np.where((lane % HEAD_DIM) < half, -1.0, 1.0)[None, :]
    rope_a = np.stack([np.cos(ang_a), np.sin(ang_a)])
    rope_b = np.stack([np.cos(ang_b), np.sin(ang_b), sign * np.cos(ang_b), sign * np.sin(ang_b)])
    return jnp.asarray(rope_a, F32), jnp.asarray(rope_b, F32)


LATE_WEIGHTS = ("w_out", "w2_gate", "w2_up", "w2_down")


def _trunk(x, w, *, n1, n2, late_f32=None):
    b, seq, _ = x.shape
    t = b * seq
    d1, d2, tw_a, tw_b = _dft_tables(n1, n2)
    rope_a, rope_b = _rope_tables(n1, n2)
    x1, casts = _ffn(x.reshape(t, D_MODEL), w["g_ffn1"], w["w1_gate"], w["w1_up"], w["w1_down"],
                     cast=tuple(late_f32[n] for n in LATE_WEIGHTS) if late_f32 else ())
    late = dict(zip(LATE_WEIGHTS, casts)) if late_f32 else {n: w[n] for n in LATE_WEIGHTS}
    t_arr, q, k, v = _inproj(x1.reshape(b, n1, n2, D_MODEL), w["g_mix"], w["w_proj"], rope_a, rope_b,
                             tw_a, tw_b, d1, n1=n1, n2=n2)
    ya = _attention(q.reshape(t, ATTN_WIDTH), k.reshape(t, KV_WIDTH), v.reshape(t, KV_WIDTH), w["attn_sink"],
                    seq=seq)
    y = _mixffn(t_arr, x1.reshape(b, n2, n1, D_MODEL), ya.reshape(b, n2, n1, ATTN_WIDTH), d2, late["w_out"],
                w["g_ffn2"], late["w2_gate"], late["w2_up"], late["w2_down"], w["g_final"], n1=n1, n2=n2)
    return y.reshape(b, seq, D_MODEL), late


def kernel(x_prompt, x_sample, g_ffn1, w1_gate, w1_up, w1_down, g_mix, w_in, w_fourier, attn_sink, w_out,
           g_ffn2, w2_gate, w2_up, w2_down, g_final):
    assert g_ffn1.shape[0] == 1, "single-layer trunk"
    w = {
        "g_ffn1": g_ffn1[0][None, :], "g_mix": g_mix[0][None, :], "g_ffn2": g_ffn2[0][None, :],
        "g_final": g_final[None, :],
        "w1_gate": w1_gate[0].astype(BF16), "w1_up": w1_up[0].astype(BF16), "w1_down": w1_down[0].astype(BF16),
        "w_proj": _fold_weights(w_in[0], w_fourier[0]),
        "attn_sink": attn_sink[0],
    }
    late_f32 = {"w_out": w_out[0], "w2_gate": w2_gate[0], "w2_up": w2_up[0], "w2_down": w2_down[0]}
    y_prompt, late = _trunk(x_prompt, w, n1=128, n2=128, late_f32=late_f32)
    w.update(late)
    y_sample, _ = _trunk(x_sample, w, n1=64, n2=64)
    return (y_prompt, y_sample)
```

```python
import functools
import math

import numpy as np
import jax
import jax.numpy as jnp
from jax import lax
from jax.experimental import pallas as pl
from jax.experimental.pallas import tpu as pltpu

D_MODEL = 1024
HEAD_DIM = 64
N_FOURIER_GROUPS = 8
FOURIER_WIDTH = N_FOURIER_GROUPS * HEAD_DIM
N_Q_HEADS = 8
N_KV_HEADS = 2
Q_PER_KV = N_Q_HEADS // N_KV_HEADS
ATTN_WIDTH = N_Q_HEADS * HEAD_DIM
KV_WIDTH = N_KV_HEADS * HEAD_DIM
D_FF = 2816
WINDOW = 128
BLOCK = 128
ROPE_THETA = 10000.0
RMS_EPS = 1e-6
NEG_INF = -1e30

SUBLANES = 8
BF16_SUBLANES = 16
LANES = 128
PROJ_WIDTH = 2 * FOURIER_WIDTH + ATTN_WIDTH + 2 * KV_WIDTH
VMEM_LIMIT_BYTES = 56 * 1024 * 1024
TOKEN_TILE = 512
INPROJ_ROWS = 256

F32 = jnp.float32
BF16 = jnp.bfloat16


def _rms(x, g):
    return x * lax.rsqrt(jnp.mean(x * x, axis=-1, keepdims=True) + RMS_EPS) * g


def _const_spec(shape):
    zeros = (0,) * len(shape)
    return pl.BlockSpec(shape, lambda *_: zeros, pipeline_mode=pl.Buffered(1))


def _params(n_axes):
    return pltpu.CompilerParams(dimension_semantics=("arbitrary",) * n_axes,
                                vmem_limit_bytes=VMEM_LIMIT_BYTES)


def _fold_kernel(w_in_ref, wf_ref, cc_ref, sc_ref, o_ref):
    hi = lax.Precision.HIGHEST
    for g in range(N_FOURIER_GROUPS):
        wf = wf_ref[g]
        pr = jnp.dot(cc_ref[...], wf, precision=hi, preferred_element_type=F32)
        pi = jnp.dot(sc_ref[...], wf, precision=hi, preferred_element_type=F32)
        wug = w_in_ref[:, g * HEAD_DIM:(g + 1) * HEAD_DIM]
        o_ref[:, g * HEAD_DIM:(g + 1) * HEAD_DIM] = jnp.dot(
            wug, pr, precision=hi, preferred_element_type=F32).astype(BF16)
        o_ref[:, FOURIER_WIDTH + g * HEAD_DIM:FOURIER_WIDTH + (g + 1) * HEAD_DIM] = (-jnp.dot(
            wug, pi, precision=hi, preferred_element_type=F32)).astype(BF16)
    o_ref[:, 2 * FOURIER_WIDTH:] = w_in_ref[:, FOURIER_WIDTH:].astype(BF16)


def _fold_weights(w_in, w_fourier):
    c = np.arange(HEAD_DIM)
    ang = 2.0 * np.pi * np.outer(c, c) / HEAD_DIM
    scale = HEAD_DIM ** -0.5
    cc = jnp.asarray(np.cos(ang) * scale, F32)
    sc = jnp.asarray(np.sin(ang) * scale, F32)
    return pl.pallas_call(
        _fold_kernel,
        out_shape=jax.ShapeDtypeStruct((D_MODEL, PROJ_WIDTH), BF16),
        name="fold",
    )(w_in, w_fourier, cc, sc)


FF_CHUNKS = ((0, 1024), (1024, 2048), (2048, D_FF))


def _swiglu(xn, wg_ref, wu_ref, wd_ref):
    acc = None
    for lo, hi in FF_CHUNKS:
        gate = jnp.dot(xn, wg_ref[:, lo:hi], preferred_element_type=F32)
        up = jnp.dot(xn, wu_ref[:, lo:hi], preferred_element_type=F32)
        act = (gate * jax.nn.sigmoid(gate) * up).astype(BF16)
        part = jnp.dot(act, wd_ref[lo:hi, :], preferred_element_type=F32)
        acc = part if acc is None else acc + part
    return acc


def _ffn_kernel(x_ref, g_ref, wg_ref, wu_ref, wd_ref, *rest):
    n_cast = len(rest) // 2
    o_ref = rest[n_cast]
    for src, dst in zip(rest[:n_cast], rest[n_cast + 1:]):
        dst[...] = src[...].astype(BF16)
    half = x_ref.shape[0] // 2
    for h in range(2):
        rows = slice(h * half, (h + 1) * half)
        x = x_ref[rows, :]
        xn = _rms(x, g_ref[...]).astype(BF16)
        gate = jnp.dot(xn, wg_ref[...], preferred_element_type=F32)
        up = jnp.dot(xn, wu_ref[...], preferred_element_type=F32)
        act = (gate * jax.nn.sigmoid(gate) * up).astype(BF16)
        o_ref[rows, :] = x + 0.5 * jnp.dot(act, wd_ref[...], preferred_element_type=F32)


def _ffn(x, g, wg, wu, wd, cast=()):
    t = x.shape[0]
    tm = TOKEN_TILE
    steps = t // tm
    cast_specs, cast_shapes = [], []
    for m in cast:
        rows, cols = m.shape
        chunks = math.gcd(rows // BF16_SUBLANES, steps)
        per = steps // chunks
        cast_specs.append(pl.BlockSpec((rows // chunks, cols), lambda i, per=per: (i // per, 0)))
        cast_shapes.append(jax.ShapeDtypeStruct(m.shape, BF16))
    out = pl.pallas_call(
        _ffn_kernel,
        grid=(steps,),
        in_specs=[
            pl.BlockSpec((tm, D_MODEL), lambda i: (i, 0)),
            _const_spec((1, D_MODEL)),
            _const_spec((D_MODEL, D_FF)),
            _const_spec((D_MODEL, D_FF)),
            _const_spec((D_FF, D_MODEL)),
            *cast_specs,
        ],
        out_specs=[pl.BlockSpec((tm, D_MODEL), lambda i: (i, 0)), *cast_specs],
        out_shape=[jax.ShapeDtypeStruct((t, D_MODEL), F32), *cast_shapes],
        compiler_params=_params(1),
        name="ffn",
    )(x, g, wg, wu, wd, *cast)
    return out[0], tuple(out[1:])


def _inproj_kernel(x_ref, g_ref, w_ref, ra_ref, rb_ref, ta_ref, tb_ref, d1_ref,
                   t_ref, q_ref, k_ref, v_ref, z_scr, *, n1):
    tmh = INPROJ_ROWS
    n1h = tmh // SUBLANES
    lane = lax.broadcasted_iota(jnp.int32, (tmh, LANES), 1)
    first_half = (lane % HEAD_DIM) < (HEAD_DIM // 2)
    q0 = 2 * FOURIER_WIDTH
    k0 = q0 + ATTN_WIDTH
    for h in range(n1 // n1h):
        s1s = slice(h * n1h, (h + 1) * n1h)
        x = x_ref[s1s].reshape(tmh, D_MODEL)
        hb = _rms(x, g_ref[...]).astype(BF16)
        proj = jnp.dot(hb, w_ref[...], preferred_element_type=F32)
        for c in range(2 * FOURIER_WIDTH // LANES):
            z_scr[c, h * tmh:(h + 1) * tmh, :] = proj[:, c * LANES:(c + 1) * LANES]

        def rows_a(t):
            return jnp.broadcast_to(t[:, None, :], (n1h, SUBLANES, LANES)).reshape(tmh, LANES)

        def rows_b(t):
            return jnp.broadcast_to(t[None, :, :], (n1h, SUBLANES, LANES)).reshape(tmh, LANES)

        ca, sa = rows_a(ra_ref[0, s1s, :]), rows_a(ra_ref[1, s1s, :])
        cos = ca * rows_b(rb_ref[0]) - sa * rows_b(rb_ref[1])
        sin = sa * rows_b(rb_ref[2]) + ca * rows_b(rb_ref[3])

        def rope(xc):
            rot = jnp.where(first_half, pltpu.roll(xc, LANES - HEAD_DIM // 2, 1), pltpu.roll(xc, HEAD_DIM // 2, 1))
            return xc * cos + rot * sin

        for c in range(ATTN_WIDTH // LANES):
            qc = rope(proj[:, q0 + c * LANES:q0 + (c + 1) * LANES]) * (HEAD_DIM ** -0.5)
            q_ref[s1s, :, c * LANES:(c + 1) * LANES] = qc.reshape(n1h, SUBLANES, LANES)
        k_ref[s1s] = rope(proj[:, k0:k0 + KV_WIDTH]).reshape(n1h, SUBLANES, KV_WIDTH)
        v_ref[s1s] = proj[:, k0 + KV_WIDTH:k0 + 2 * KV_WIDTH].reshape(n1h, SUBLANES, KV_WIDTH)

    d1 = d1_ref[...]
    for j in range(SUBLANES):
        rows = pl.ds(j, n1, stride=SUBLANES)
        nch = FOURIER_WIDTH // LANES
        zr = jnp.concatenate([z_scr[c, rows, :] for c in range(nch)], axis=1)
        zi = jnp.concatenate([z_scr[nch + c, rows, :] for c in range(nch)], axis=1)
        st = jnp.concatenate([zr, zi], axis=0).astype(BF16)
        t = jnp.dot(d1, st, preferred_element_type=F32)
        tr, ti = t[:n1], t[n1:]
        twc = ta_ref[0] * tb_ref[0, j] - ta_ref[1] * tb_ref[1, j]
        tws = ta_ref[1] * tb_ref[0, j] + ta_ref[0] * tb_ref[1, j]
        c4 = jnp.concatenate([twc] * (FOURIER_WIDTH // LANES), axis=1)
        s4 = jnp.concatenate([tws] * (FOURIER_WIDTH // LANES), axis=1)
        t_re, t_im = tr * c4 + ti * s4, ti * c4 - tr * s4
        for c in range(FOURIER_WIDTH // LANES):
            t_ref[c, j, :n1, :] = t_re[:, c * LANES:(c + 1) * LANES]
            t_ref[c, j, n1:, :] = t_im[:, c * LANES:(c + 1) * LANES]


def _inproj(x1, g, w, rope_a, rope_b, tw_a, tw_b, d1, *, n1, n2):
    b = x1.shape[0]
    blk = lambda w_: pl.BlockSpec((None, n1, SUBLANES, w_), lambda bi, i: (bi, 0, i, 0))
    tab_b = pl.BlockSpec((4, SUBLANES, LANES), lambda bi, i: (0, i, 0))
    tw_a_spec = pl.BlockSpec((2, None, n1, LANES), lambda bi, i: (0, i, 0, 0))
    return pl.pallas_call(
        functools.partial(_inproj_kernel, n1=n1),
        grid=(b, n2 // SUBLANES),
        in_specs=[blk(D_MODEL), _const_spec((1, D_MODEL)), _const_spec((D_MODEL, PROJ_WIDTH)),
                  _const_spec((2, n1, LANES)), tab_b, tw_a_spec, _const_spec((2, SUBLANES, n1, LANES)),
                  _const_spec((2 * n1, 2 * n1))],
        out_specs=[
            pl.BlockSpec((None, FOURIER_WIDTH // LANES, SUBLANES, 2 * n1, LANES), lambda bi, i: (bi, 0, i, 0, 0)),
            blk(ATTN_WIDTH), blk(KV_WIDTH), blk(KV_WIDTH),
        ],
        out_shape=[
            jax.ShapeDtypeStruct((b, FOURIER_WIDTH // LANES, n2, 2 * n1, LANES), F32),
            jax.ShapeDtypeStruct((b, n1, n2, ATTN_WIDTH), F32),
            jax.ShapeDtypeStruct((b, n1, n2, KV_WIDTH), F32),
            jax.ShapeDtypeStruct((b, n1, n2, KV_WIDTH), F32),
        ],
        scratch_shapes=[pltpu.VMEM((2 * FOURIER_WIDTH // LANES, n1 * SUBLANES, LANES), F32)],
        compiler_params=_params(2),
        name="inproj",
    )(x1, g, w, rope_a, rope_b, tw_a, tw_b, d1)


F32_MAX = float(np.finfo(np.float32).max)
GROUP_WIDTH = Q_PER_KV * HEAD_DIM


def _attn_kernel(sink_ref, q_ref, kp_ref, kc_ref, kn_ref, vp_ref, vc_ref, vn_ref, o_ref, *, nb, nblk_seq):
    i = pl.program_id(0)
    rows = (nb + 2) * BLOCK
    kfull = jnp.concatenate([kp_ref[...], kc_ref[...], kn_ref[...]], axis=0)
    vfull = jnp.concatenate([vp_ref[...], vc_ref[...], vn_ref[...]], axis=0)
    lo = lax.broadcasted_iota(jnp.int32, (rows, LANES), 1) < HEAD_DIM

    def dup(x):
        r = pltpu.roll(x, HEAD_DIM, 1)
        return jnp.where(lo, x, r), jnp.where(lo, r, x)

    kt = jnp.transpose(kfull).astype(BF16)
    zero_kt = jnp.zeros((HEAD_DIM, 3 * BLOCK), BF16)
    vv = dup(vfull)
    wkeys = 3 * BLOCK
    grp = lax.broadcasted_iota(jnp.int32, (wkeys, GROUP_WIDTH), 1) // HEAD_DIM
    ogrp = lax.broadcasted_iota(jnp.int32, (BLOCK, GROUP_WIDTH), 1) // HEAD_DIM
    kidx = lax.broadcasted_iota(jnp.int32, (BLOCK, BLOCK), 1)
    qidx = lax.broadcasted_iota(jnp.int32, (BLOCK, BLOCK), 0)
    cap_first = jnp.where(kidx >= qidx, F32_MAX, NEG_INF)
    cap_last = jnp.where(kidx <= qidx, F32_MAX, NEG_INF)

    def blockdiag(x):
        wide = jnp.concatenate([x, x], axis=1)
        return jnp.concatenate([jnp.where(grp == g, wide, 0.0) for g in range(Q_PER_KV)], axis=0).astype(BF16)

    def blockdiag_t(xt):
        return jnp.concatenate(
            [jnp.concatenate([xt if gg == g else zero_kt for gg in range(Q_PER_KV)], axis=1)
             for g in range(Q_PER_KV)], axis=0)

    caps = []
    for n in range(nb):
        pb = (i * nb + n) % nblk_seq
        caps.append((jnp.minimum(cap_first, jnp.where(pb > 0, F32_MAX, NEG_INF)),
                     jnp.minimum(cap_last, jnp.where(pb < nblk_seq - 1, F32_MAX, NEG_INF))))

    def scores(n, kh):
        kbd_t = blockdiag_t(kt[kh * HEAD_DIM:(kh + 1) * HEAD_DIM, n * BLOCK:n * BLOCK + wkeys])
        c0 = kh * GROUP_WIDTH
        qh = q_ref[n * BLOCK:(n + 1) * BLOCK, c0:c0 + GROUP_WIDTH].astype(BF16)
        return jnp.dot(qh, kbd_t, preferred_element_type=F32)

    def softmax(n, kh, s_all):
        capf, capl = caps[n]
        tiles, maxes = [], []
        for g in range(Q_PER_KV):
            s0 = jnp.minimum(s_all[:, g * wkeys:g * wkeys + BLOCK], capf)
            s1 = s_all[:, g * wkeys + BLOCK:g * wkeys + 2 * BLOCK]
            s2 = jnp.minimum(s_all[:, g * wkeys + 2 * BLOCK:(g + 1) * wkeys], capl)
            tiles.append((s0, s1, s2))
            maxes.append(jnp.max(jnp.maximum(jnp.maximum(s0, s1), s2), axis=-1, keepdims=True))
        ps, dens = [], []
        for g in range(Q_PER_KV):
            p = [jnp.exp((t - maxes[g]).astype(BF16)) for t in tiles[g]]
            psum = (p[0] + p[1] + p[2]).astype(F32)
            dens.append(jnp.sum(psum, axis=-1, keepdims=True) + jnp.exp(sink_ref[kh * Q_PER_KV + g] - maxes[g]))
            ps += p
        scale = 1.0 / dens[Q_PER_KV - 1]
        for g in range(Q_PER_KV - 2, -1, -1):
            scale = jnp.where(ogrp == g, 1.0 / dens[g], scale)
        return jnp.concatenate(ps, axis=1), scale

    def output(n, kh, p_all, scale):
        vbd = blockdiag(vv[kh][n * BLOCK:n * BLOCK + wkeys])
        c0 = kh * GROUP_WIDTH
        o_ref[n * BLOCK:(n + 1) * BLOCK, c0:c0 + GROUP_WIDTH] = jnp.dot(
            p_all, vbd, preferred_element_type=F32) * scale

    units = [(n, kh) for kh in range(N_KV_HEADS) for n in range(nb)]
    s_prev = None
    for k in range(len(units) + 1):
        s_cur = scores(*units[k]) if k < len(units) else None
        if s_prev is not None:
            output(*units[k - 1], *softmax(*units[k - 1], s_prev))
        s_prev = s_cur


def _attention(q, k, v, sink, *, seq, nb=16):
    t = q.shape[0]
    nblk = t // BLOCK
    tq = nb * BLOCK
    cur = lambda w_: pl.BlockSpec((tq, w_), lambda i: (i, 0))
    prev = pl.BlockSpec((BLOCK, KV_WIDTH), lambda i: (jnp.maximum(i * nb - 1, 0), 0))
    nxt = pl.BlockSpec((BLOCK, KV_WIDTH), lambda i: (jnp.minimum(i * nb + nb, nblk - 1), 0))
    return pl.pallas_call(
        functools.partial(_attn_kernel, nb=nb, nblk_seq=seq // BLOCK),
        grid=(t // tq,),
        in_specs=[pl.BlockSpec(memory_space=pltpu.SMEM), cur(ATTN_WIDTH),
                  prev, cur(KV_WIDTH), nxt, prev, cur(KV_WIDTH), nxt],
        out_specs=cur(ATTN_WIDTH),
        out_shape=jax.ShapeDtypeStruct((t, ATTN_WIDTH), F32),
        compiler_params=_params(1),
        name="attention",
    )(sink, q, k, k, k, v, v, v)


def _mixffn_kernel(tr_ref, ti_ref, x_ref, ya_ref, d2_ref, wo_ref, g2_ref, wg_ref, wu_ref, wd_ref, gf_ref,
                   o_ref, yf_scr, *, n2, n2h):
    tm = n2h * SUBLANES
    nch = FOURIER_WIDTH // LANES

    d2 = d2_ref[...]
    tr_flat = tr_ref.reshape(nch, n2 * SUBLANES, LANES)
    ti_flat = ti_ref.reshape(nch, n2 * SUBLANES, LANES)
    for j in range(SUBLANES):
        rows = pl.ds(j, n2, stride=SUBLANES)
        tr = jnp.concatenate([tr_flat[c, rows, :] for c in range(nch)], axis=1)
        ti = jnp.concatenate([ti_flat[c, rows, :] for c in range(nch)], axis=1)
        st = jnp.concatenate([tr, ti], axis=0).astype(BF16)
        yfj = jnp.dot(d2, st, preferred_element_type=F32)
        for c in range(nch):
            yf_scr[c, pl.ds(j, n2h, stride=SUBLANES), :] = yfj[:, c * LANES:(c + 1) * LANES]
    yf = jnp.concatenate([yf_scr[c] for c in range(nch)], axis=1).astype(BF16)
    ya = ya_ref[...].reshape(tm, ATTN_WIDTH).astype(BF16)
    x2 = (x_ref[...].reshape(tm, D_MODEL)
          + jnp.dot(yf, wo_ref[:FOURIER_WIDTH, :], preferred_element_type=F32)
          + jnp.dot(ya, wo_ref[FOURIER_WIDTH:, :], preferred_element_type=F32))

    half = tm // 2
    for h in range(2):
        xh = x2[h * half:(h + 1) * half]
        xn = _rms(xh, g2_ref[...]).astype(BF16)
        y = _rms(xh + 0.5 * _swiglu(xn, wg_ref, wu_ref, wd_ref), gf_ref[...])
        o_ref[h * (n2h // 2):(h + 1) * (n2h // 2)] = y.reshape(n2h // 2, SUBLANES, D_MODEL)


def _mixffn(t_arr, x1, ya, d2, wo, g2, wg, wu, wd, gf, *, n1, n2):
    b = x1.shape[0]
    halves = (n2 * SUBLANES) // TOKEN_TILE
    n2h = n2 // halves
    t_shape = (None, FOURIER_WIDTH // LANES, n2, SUBLANES, LANES)
    t_re = pl.BlockSpec(t_shape, lambda bi, i, h: (bi, 0, 0, i, 0))
    t_im = pl.BlockSpec(t_shape, lambda bi, i, h: (bi, 0, 0, n1 // SUBLANES + i, 0))
    blk = lambda w_: pl.BlockSpec((None, n2h, SUBLANES, w_), lambda bi, i, h: (bi, h, i, 0))
    return pl.pallas_call(
        functools.partial(_mixffn_kernel, n2=n2, n2h=n2h),
        grid=(b, n1 // SUBLANES, halves),
        in_specs=[t_re, t_im, blk(D_MODEL), blk(ATTN_WIDTH),
                  pl.BlockSpec((n2h, 2 * n2), lambda bi, i, h: (h, 0)),
                  _const_spec((2 * FOURIER_WIDTH, D_MODEL)), _const_spec((1, D_MODEL)),
                  _const_spec((D_MODEL, D_FF)), _const_spec((D_MODEL, D_FF)), _const_spec((D_FF, D_MODEL)),
                  _const_spec((1, D_MODEL))],
        out_specs=blk(D_MODEL),
        out_shape=jax.ShapeDtypeStruct((b, n2, n1, D_MODEL), F32),
        scratch_shapes=[pltpu.VMEM((FOURIER_WIDTH // LANES, n2h * SUBLANES, LANES), F32)],
        compiler_params=_params(3),
        name="mixffn",
    )(t_arr, t_arr, x1, ya, d2, wo, g2, wg, wu, wd, gf)


def _dft_tables(n1, n2):
    s = n1 * n2
    a1 = 2.0 * np.pi * np.outer(np.arange(n1), np.arange(n1)) / n1
    c1, s1 = np.cos(a1) / np.sqrt(n1), np.sin(a1) / np.sqrt(n1)
    d1 = np.block([[c1, s1], [-s1, c1]])
    a2 = 2.0 * np.pi * np.outer(np.arange(n2), np.arange(n2)) / n2
    d2 = np.concatenate([np.cos(a2), np.sin(a2)], axis=1) / np.sqrt(n2)
    k1 = jnp.arange(n1, dtype=jnp.int32)[None, :]

    def table(mult, count):
        m = (jnp.arange(count, dtype=jnp.int32)[:, None] * mult * k1) % s
        ang = m.astype(F32) * (2.0 * np.pi / s)
        return jnp.broadcast_to(jnp.stack([jnp.cos(ang), jnp.sin(ang)])[..., None], (2, count, n1, LANES))

    tw_a, tw_b = table(SUBLANES, n2 // SUBLANES), table(1, SUBLANES)
    return jnp.asarray(d1, F32).astype(BF16), jnp.asarray(d2, F32).astype(BF16), tw_a, tw_b


def _rope_tables(n1, n2):
    half = HEAD_DIM // 2
    lane = np.arange(LANES)
    inv_freq = ROPE_THETA ** (-(lane % half) / half)
    ang_a = (np.arange(n1) * n2)[:, None] * inv_freq[None, :]
    ang_b = np.arange(n2)[:, None] * inv_freq[None, :]
    sign = np.where((lane % HEAD_DIM) < half, -1.0, 1.0)[None, :]
    rope_a = np.stack([np.cos(ang_a), np.sin(ang_a)])
    rope_b = np.stack([np.cos(ang_b), np.sin(ang_b), sign * np.cos(ang_b), sign * np.sin(ang_b)])
    return jnp.asarray(rope_a, F32), jnp.asarray(rope_b, F32)


LATE_WEIGHTS = ("w_out", "w2_gate", "w2_up", "w2_down")


def _trunk(x, w, *, n1, n2, late_f32=None):
    b, seq, _ = x.shape
    t = b * seq
    d1, d2, tw_a, tw_b = _dft_tables(n1, n2)
    rope_a, rope_b = _rope_tables(n1, n2)
    x1, casts = _ffn(x.reshape(t, D_MODEL), w["g_ffn1"], w["w1_gate"], w["w1_up"], w["w1_down"],
                     cast=tuple(late_f32[n] for n in LATE_WEIGHTS) if late_f32 else ())
    late = dict(zip(LATE_WEIGHTS, casts)) if late_f32 else {n: w[n] for n in LATE_WEIGHTS}
    t_arr, q, k, v = _inproj(x1.reshape(b, n1, n2, D_MODEL), w["g_mix"], w["w_proj"], rope_a, rope_b,
                             tw_a, tw_b, d1, n1=n1, n2=n2)
    ya = _attention(q.reshape(t, ATTN_WIDTH), k.reshape(t, KV_WIDTH), v.reshape(t, KV_WIDTH), w["attn_sink"],
                    seq=seq)
    y = _mixffn(t_arr, x1.reshape(b, n2, n1, D_MODEL), ya.reshape(b, n2, n1, ATTN_WIDTH), d2, late["w_out"],
                w["g_ffn2"], late["w2_gate"], late["w2_up"], late["w2_down"], w["g_final"], n1=n1, n2=n2)
    return y.reshape(b, seq, D_MODEL), late


def kernel(x_prompt, x_sample, g_ffn1, w1_gate, w1_up, w1_down, g_mix, w_in, w_fourier, attn_sink, w_out,
           g_ffn2, w2_gate, w2_up, w2_down, g_final):
    assert g_ffn1.shape[0] == 1, "single-layer trunk"
    w = {
        "g_ffn1": g_ffn1[0][None, :], "g_mix": g_mix[0][None, :], "g_ffn2": g_ffn2[0][None, :],
        "g_final": g_final[None, :],
        "w1_gate": w1_gate[0].astype(BF16), "w1_up": w1_up[0].astype(BF16), "w1_down": w1_down[0].astype(BF16),
        "w_proj": _fold_weights(w_in[0], w_fourier[0]),
        "attn_sink": attn_sink[0],
    }
    late_f32 = {"w_out": w_out[0], "w2_gate": w2_gate[0], "w2_up": w2_up[0], "w2_down": w2_down[0]}
    y_prompt, late = _trunk(x_prompt, w, n1=128, n2=128, late_f32=late_f32)
    w.update(late)
    y_sample, _ = _trunk(x_sample, w, n1=64, n2=64)
    return (y_prompt, y_sample)
```

```python
import functools
import math

import numpy as np
import jax
import jax.numpy as jnp
from jax import lax
from jax.experimental import pallas as pl
from jax.experimental.pallas import tpu as pltpu

D_MODEL = 1024
HEAD_DIM = 64
N_FOURIER_GROUPS = 8
FOURIER_WIDTH = N_FOURIER_GROUPS * HEAD_DIM
N_Q_HEADS = 8
N_KV_HEADS = 2
Q_PER_KV = N_Q_HEADS // N_KV_HEADS
ATTN_WIDTH = N_Q_HEADS * HEAD_DIM
KV_WIDTH = N_KV_HEADS * HEAD_DIM
D_FF = 2816
WINDOW = 128
BLOCK = 128
assert WINDOW == BLOCK, "the band caps in the attention kernel are written for WINDOW == BLOCK"
ROPE_THETA = 10000.0
RMS_EPS = 1e-6
NEG_INF = -1e30

SUBLANES = 8
BF16_SUBLANES = 16
LANES = 128
PROJ_WIDTH = 2 * FOURIER_WIDTH + ATTN_WIDTH + 2 * KV_WIDTH
VMEM_LIMIT_BYTES = 56 * 1024 * 1024
TOKEN_TILE = 512
INPROJ_ROWS = 256

F32 = jnp.float32
BF16 = jnp.bfloat16


def _rms(x, g):
    return x * lax.rsqrt(jnp.mean(x * x, axis=-1, keepdims=True) + RMS_EPS) * g


def _const_spec(shape):
    zeros = (0,) * len(shape)
    return pl.BlockSpec(shape, lambda *_: zeros, pipeline_mode=pl.Buffered(1))


def _params(n_axes):
    return pltpu.CompilerParams(dimension_semantics=("arbitrary",) * n_axes,
                                vmem_limit_bytes=VMEM_LIMIT_BYTES)


def _fold_kernel(w_in_ref, wf_ref, cc_ref, sc_ref, o_ref):
    hi = lax.Precision.HIGHEST
    for g in range(N_FOURIER_GROUPS):
        wf = wf_ref[g]
        pr = jnp.dot(cc_ref[...], wf, precision=hi, preferred_element_type=F32)
        pi = jnp.dot(sc_ref[...], wf, precision=hi, preferred_element_type=F32)
        wug = w_in_ref[:, g * HEAD_DIM:(g + 1) * HEAD_DIM]
        o_ref[:, g * HEAD_DIM:(g + 1) * HEAD_DIM] = jnp.dot(
            wug, pr, precision=hi, preferred_element_type=F32).astype(BF16)
        o_ref[:, FOURIER_WIDTH + g * HEAD_DIM:FOURIER_WIDTH + (g + 1) * HEAD_DIM] = (-jnp.dot(
            wug, pi, precision=hi, preferred_element_type=F32)).astype(BF16)
    o_ref[:, 2 * FOURIER_WIDTH:] = w_in_ref[:, FOURIER_WIDTH:].astype(BF16)


def _fold_weights(w_in, w_fourier):
    c = np.arange(HEAD_DIM)
    ang = 2.0 * np.pi * np.outer(c, c) / HEAD_DIM
    scale = HEAD_DIM ** -0.5
    cc = jnp.asarray(np.cos(ang) * scale, F32)
    sc = jnp.asarray(np.sin(ang) * scale, F32)
    return pl.pallas_call(
        _fold_kernel,
        out_shape=jax.ShapeDtypeStruct((D_MODEL, PROJ_WIDTH), BF16),
        name="fold",
    )(w_in, w_fourier, cc, sc)


FF_CHUNKS = ((0, 1024), (1024, 2048), (2048, D_FF))


def _swiglu(xn, wg_ref, wu_ref, wd_ref):
    acc = None
    for lo, hi in FF_CHUNKS:
        gate = jnp.dot(xn, wg_ref[:, lo:hi], preferred_element_type=F32)
        up = jnp.dot(xn, wu_ref[:, lo:hi], preferred_element_type=F32)
        act = (gate * jax.nn.sigmoid(gate) * up).astype(BF16)
        part = jnp.dot(act, wd_ref[lo:hi, :], preferred_element_type=F32)
        acc = part if acc is None else acc + part
    return acc


def _ffn_kernel(x_ref, g_ref, wg_ref, wu_ref, wd_ref, *rest):
    n_cast = len(rest) // 2
    o_ref = rest[n_cast]
    for src, dst in zip(rest[:n_cast], rest[n_cast + 1:]):
        dst[...] = src[...].astype(BF16)
    half = x_ref.shape[0] // 2
    for h in range(2):
        rows = slice(h * half, (h + 1) * half)
        x = x_ref[rows, :]
        xn = _rms(x, g_ref[...]).astype(BF16)
        gate = jnp.dot(xn, wg_ref[...], preferred_element_type=F32)
        up = jnp.dot(xn, wu_ref[...], preferred_element_type=F32)
        act = (gate * jax.nn.sigmoid(gate) * up).astype(BF16)
        o_ref[rows, :] = x + 0.5 * jnp.dot(act, wd_ref[...], preferred_element_type=F32)


def _ffn(x, g, wg, wu, wd, cast=()):
    t = x.shape[0]
    tm = TOKEN_TILE
    steps = t // tm
    cast_specs, cast_shapes = [], []
    for m in cast:
        rows, cols = m.shape
        chunks = math.gcd(rows // BF16_SUBLANES, steps)
        per = steps // chunks
        cast_specs.append(pl.BlockSpec((rows // chunks, cols), lambda i, per=per: (i // per, 0)))
        cast_shapes.append(jax.ShapeDtypeStruct(m.shape, BF16))
    out = pl.pallas_call(
        _ffn_kernel,
        grid=(steps,),
        in_specs=[
            pl.BlockSpec((tm, D_MODEL), lambda i: (i, 0)),
            _const_spec((1, D_MODEL)),
            _const_spec((D_MODEL, D_FF)),
            _const_spec((D_MODEL, D_FF)),
            _const_spec((D_FF, D_MODEL)),
            *cast_specs,
        ],
        out_specs=[pl.BlockSpec((tm, D_MODEL), lambda i: (i, 0)), *cast_specs],
        out_shape=[jax.ShapeDtypeStruct((t, D_MODEL), F32), *cast_shapes],
        compiler_params=_params(1),
        name="ffn",
    )(x, g, wg, wu, wd, *cast)
    return out[0], tuple(out[1:])


def _inproj_kernel(x_ref, g_ref, w_ref, ra_ref, rb_ref, ta_ref, tb_ref, d1_ref,
                   t_ref, q_ref, k_ref, v_ref, z_scr, *, n1):
    tmh = INPROJ_ROWS
    n1h = tmh // SUBLANES
    lane = lax.broadcasted_iota(jnp.int32, (tmh, LANES), 1)
    first_half = (lane % HEAD_DIM) < (HEAD_DIM // 2)
    q0 = 2 * FOURIER_WIDTH
    k0 = q0 + ATTN_WIDTH
    for h in range(n1 // n1h):
        s1s = slice(h * n1h, (h + 1) * n1h)
        x = x_ref[s1s].reshape(tmh, D_MODEL)
        hb = _rms(x, g_ref[...]).astype(BF16)
        proj = jnp.dot(hb, w_ref[...], preferred_element_type=F32)
        for c in range(2 * FOURIER_WIDTH // LANES):
            z_scr[c, h * tmh:(h + 1) * tmh, :] = proj[:, c * LANES:(c + 1) * LANES]

        def rows_a(t):
            return jnp.broadcast_to(t[:, None, :], (n1h, SUBLANES, LANES)).reshape(tmh, LANES)

        def rows_b(t):
            return jnp.broadcast_to(t[None, :, :], (n1h, SUBLANES, LANES)).reshape(tmh, LANES)

        ca, sa = rows_a(ra_ref[0, s1s, :]), rows_a(ra_ref[1, s1s, :])
        cos = ca * rows_b(rb_ref[0]) - sa * rows_b(rb_ref[1])
        sin = sa * rows_b(rb_ref[2]) + ca * rows_b(rb_ref[3])

        def rope(xc):
            rot = jnp.where(first_half, pltpu.roll(xc, LANES - HEAD_DIM // 2, 1), pltpu.roll(xc, HEAD_DIM // 2, 1))
            return xc * cos + rot * sin

        for c in range(ATTN_WIDTH // LANES):
            qc = rope(proj[:, q0 + c * LANES:q0 + (c + 1) * LANES]) * (HEAD_DIM ** -0.5)
            q_ref[s1s, :, c * LANES:(c + 1) * LANES] = qc.reshape(n1h, SUBLANES, LANES)
        k_ref[s1s] = rope(proj[:, k0:k0 + KV_WIDTH]).reshape(n1h, SUBLANES, KV_WIDTH)
        v_ref[s1s] = proj[:, k0 + KV_WIDTH:k0 + 2 * KV_WIDTH].reshape(n1h, SUBLANES, KV_WIDTH)

    d1 = d1_ref[...]
    for j in range(SUBLANES):
        rows = pl.ds(j, n1, stride=SUBLANES)
        nch = FOURIER_WIDTH // LANES
        zr = jnp.concatenate([z_scr[c, rows, :] for c in range(nch)], axis=1)
        zi = jnp.concatenate([z_scr[nch + c, rows, :] for c in range(nch)], axis=1)
        st = jnp.concatenate([zr, zi], axis=0).astype(BF16)
        t = jnp.dot(d1, st, preferred_element_type=F32)
        tr, ti = t[:n1], t[n1:]
        twc = ta_ref[0] * tb_ref[0, j] - ta_ref[1] * tb_ref[1, j]
        tws = ta_ref[1] * tb_ref[0, j] + ta_ref[0] * tb_ref[1, j]
        c4 = jnp.concatenate([twc] * (FOURIER_WIDTH // LANES), axis=1)
        s4 = jnp.concatenate([tws] * (FOURIER_WIDTH // LANES), axis=1)
        t_re, t_im = tr * c4 + ti * s4, ti * c4 - tr * s4
        for c in range(FOURIER_WIDTH // LANES):
            t_ref[c, j, :n1, :] = t_re[:, c * LANES:(c + 1) * LANES]
            t_ref[c, j, n1:, :] = t_im[:, c * LANES:(c + 1) * LANES]


def _inproj(x1, g, w, rope_a, rope_b, tw_a, tw_b, d1, *, n1, n2):
    b = x1.shape[0]
    blk = lambda w_: pl.BlockSpec((None, n1, SUBLANES, w_), lambda bi, i: (bi, 0, i, 0))
    tab_b = pl.BlockSpec((4, SUBLANES, LANES), lambda bi, i: (0, i, 0))
    tw_a_spec = pl.BlockSpec((2, None, n1, LANES), lambda bi, i: (0, i, 0, 0))
    return pl.pallas_call(
        functools.partial(_inproj_kernel, n1=n1),
        grid=(b, n2 // SUBLANES),
        in_specs=[blk(D_MODEL), _const_spec((1, D_MODEL)), _const_spec((D_MODEL, PROJ_WIDTH)),
                  _const_spec((2, n1, LANES)), tab_b, tw_a_spec, _const_spec((2, SUBLANES, n1, LANES)),
                  _const_spec((2 * n1, 2 * n1))],
        out_specs=[
            pl.BlockSpec((None, FOURIER_WIDTH // LANES, SUBLANES, 2 * n1, LANES), lambda bi, i: (bi, 0, i, 0, 0)),
            blk(ATTN_WIDTH), blk(KV_WIDTH), blk(KV_WIDTH),
        ],
        out_shape=[
            jax.ShapeDtypeStruct((b, FOURIER_WIDTH // LANES, n2, 2 * n1, LANES), F32),
            jax.ShapeDtypeStruct((b, n1, n2, ATTN_WIDTH), F32),
            jax.ShapeDtypeStruct((b, n1, n2, KV_WIDTH), F32),
            jax.ShapeDtypeStruct((b, n1, n2, KV_WIDTH), F32),
        ],
        scratch_shapes=[pltpu.VMEM((2 * FOURIER_WIDTH // LANES, n1 * SUBLANES, LANES), F32)],
        compiler_params=_params(2),
        name="inproj",
    )(x1, g, w, rope_a, rope_b, tw_a, tw_b, d1)


F32_MAX = float(np.finfo(np.float32).max)
GROUP_WIDTH = Q_PER_KV * HEAD_DIM


def _attn_kernel(sink_ref, q_ref, kp_ref, kc_ref, kn_ref, vp_ref, vc_ref, vn_ref, o_ref, *, nb, nblk_seq):
    i = pl.program_id(0)
    rows = (nb + 2) * BLOCK
    kfull = jnp.concatenate([kp_ref[...], kc_ref[...], kn_ref[...]], axis=0)
    vfull = jnp.concatenate([vp_ref[...], vc_ref[...], vn_ref[...]], axis=0)
    lo = lax.broadcasted_iota(jnp.int32, (rows, LANES), 1) < HEAD_DIM

    def dup(x):
        r = pltpu.roll(x, HEAD_DIM, 1)
        return jnp.where(lo, x, r), jnp.where(lo, r, x)

    kt = jnp.transpose(kfull).astype(BF16)
    zero_kt = jnp.zeros((HEAD_DIM, 3 * BLOCK), BF16)
    vv = dup(vfull)
    wkeys = 3 * BLOCK
    grp = lax.broadcasted_iota(jnp.int32, (wkeys, GROUP_WIDTH), 1) // HEAD_DIM
    ogrp = lax.broadcasted_iota(jnp.int32, (BLOCK, GROUP_WIDTH), 1) // HEAD_DIM
    kidx = lax.broadcasted_iota(jnp.int32, (BLOCK, BLOCK), 1)
    qidx = lax.broadcasted_iota(jnp.int32, (BLOCK, BLOCK), 0)
    cap_first = jnp.where(kidx >= qidx, F32_MAX, NEG_INF)
    cap_last = jnp.where(kidx <= qidx, F32_MAX, NEG_INF)

    def blockdiag(x):
        wide = jnp.concatenate([x, x], axis=1)
        return jnp.concatenate([jnp.where(grp == g, wide, 0.0) for g in range(Q_PER_KV)], axis=0).astype(BF16)

    def blockdiag_t(xt):
        return jnp.concatenate(
            [jnp.concatenate([xt if gg == g else zero_kt for gg in range(Q_PER_KV)], axis=1)
             for g in range(Q_PER_KV)], axis=0)

    caps = []
    for n in range(nb):
        pb = (i * nb + n) % nblk_seq
        caps.append((jnp.minimum(cap_first, jnp.where(pb > 0, F32_MAX, NEG_INF)),
                     jnp.minimum(cap_last, jnp.where(pb < nblk_seq - 1, F32_MAX, NEG_INF))))

    def scores(n, kh):
        kbd_t = blockdiag_t(kt[kh * HEAD_DIM:(kh + 1) * HEAD_DIM, n * BLOCK:n * BLOCK + wkeys])
        c0 = kh * GROUP_WIDTH
        qh = q_ref[n * BLOCK:(n + 1) * BLOCK, c0:c0 + GROUP_WIDTH].astype(BF16)
        return jnp.dot(qh, kbd_t, preferred_element_type=F32)

    def softmax(n, kh, s_all):
        capf, capl = caps[n]
        tiles, maxes = [], []
        for g in range(Q_PER_KV):
            s0 = jnp.minimum(s_all[:, g * wkeys:g * wkeys + BLOCK], capf)
            s1 = s_all[:, g * wkeys + BLOCK:g * wkeys + 2 * BLOCK]
            s2 = jnp.minimum(s_all[:, g * wkeys + 2 * BLOCK:(g + 1) * wkeys], capl)
            tiles.append((s0, s1, s2))
            maxes.append(jnp.max(jnp.maximum(jnp.maximum(s0, s1), s2), axis=-1, keepdims=True))
        ps, dens = [], []
        for g in range(Q_PER_KV):
            p = [jnp.exp((t - maxes[g]).astype(BF16)) for t in tiles[g]]
            psum = (p[0] + p[1] + p[2]).astype(F32)
            dens.append(jnp.sum(psum, axis=-1, keepdims=True) + jnp.exp(sink_ref[kh * Q_PER_KV + g] - maxes[g]))
            ps += p
        scale = 1.0 / dens[Q_PER_KV - 1]
        for g in range(Q_PER_KV - 2, -1, -1):
            scale = jnp.where(ogrp == g, 1.0 / dens[g], scale)
        return jnp.concatenate(ps, axis=1), scale

    def output(n, kh, p_all, scale):
        vbd = blockdiag(vv[kh][n * BLOCK:n * BLOCK + wkeys])
        c0 = kh * GROUP_WIDTH
        o_ref[n * BLOCK:(n + 1) * BLOCK, c0:c0 + GROUP_WIDTH] = jnp.dot(
            p_all, vbd, preferred_element_type=F32) * scale

    units = [(n, kh) for kh in range(N_KV_HEADS) for n in range(nb)]
    s_prev = None
    for k in range(len(units) + 1):
        s_cur = scores(*units[k]) if k < len(units) else None
        if s_prev is not None:
            output(*units[k - 1], *softmax(*units[k - 1], s_prev))
        s_prev = s_cur


def _attention(q, k, v, sink, *, seq, nb=16):
    t = q.shape[0]
    nblk = t // BLOCK
    tq = nb * BLOCK
    cur = lambda w_: pl.BlockSpec((tq, w_), lambda i: (i, 0))
    prev = pl.BlockSpec((BLOCK, KV_WIDTH), lambda i: (jnp.maximum(i * nb - 1, 0), 0))
    nxt = pl.BlockSpec((BLOCK, KV_WIDTH), lambda i: (jnp.minimum(i * nb + nb, nblk - 1), 0))
    return pl.pallas_call(
        functools.partial(_attn_kernel, nb=nb, nblk_seq=seq // BLOCK),
        grid=(t // tq,),
        in_specs=[pl.BlockSpec(memory_space=pltpu.SMEM), cur(ATTN_WIDTH),
                  prev, cur(KV_WIDTH), nxt, prev, cur(KV_WIDTH), nxt],
        out_specs=cur(ATTN_WIDTH),
        out_shape=jax.ShapeDtypeStruct((t, ATTN_WIDTH), F32),
        compiler_params=_params(1),
        name="attention",
    )(sink, q, k, k, k, v, v, v)


def _mixffn_kernel(tr_ref, ti_ref, x_ref, ya_ref, d2_ref, wo_ref, g2_ref, wg_ref, wu_ref, wd_ref, gf_ref,
                   o_ref, yf_scr, *, n2, n2h):
    tm = n2h * SUBLANES
    nch = FOURIER_WIDTH // LANES

    d2 = d2_ref[...]
    tr_flat = tr_ref.reshape(nch, n2 * SUBLANES, LANES)
    ti_flat = ti_ref.reshape(nch, n2 * SUBLANES, LANES)
    for j in range(SUBLANES):
        rows = pl.ds(j, n2, stride=SUBLANES)
        tr = jnp.concatenate([tr_flat[c, rows, :] for c in range(nch)], axis=1)
        ti = jnp.concatenate([ti_flat[c, rows, :] for c in range(nch)], axis=1)
        st = jnp.concatenate([tr, ti], axis=0).astype(BF16)
        yfj = jnp.dot(d2, st, preferred_element_type=F32)
        for c in range(nch):
            yf_scr[c, pl.ds(j, n2h, stride=SUBLANES), :] = yfj[:, c * LANES:(c + 1) * LANES]
    yf = jnp.concatenate([yf_scr[c] for c in range(nch)], axis=1).astype(BF16)
    ya = ya_ref[...].reshape(tm, ATTN_WIDTH).astype(BF16)
    x2 = (x_ref[...].reshape(tm, D_MODEL)
          + jnp.dot(yf, wo_ref[:FOURIER_WIDTH, :], preferred_element_type=F32)
          + jnp.dot(ya, wo_ref[FOURIER_WIDTH:, :], preferred_element_type=F32))

    half = tm // 2
    for h in range(2):
        xh = x2[h * half:(h + 1) * half]
        xn = _rms(xh, g2_ref[...]).astype(BF16)
        y = _rms(xh + 0.5 * _swiglu(xn, wg_ref, wu_ref, wd_ref), gf_ref[...])
        o_ref[h * (n2h // 2):(h + 1) * (n2h // 2)] = y.reshape(n2h // 2, SUBLANES, D_MODEL)


def _mixffn(t_arr, x1, ya, d2, wo, g2, wg, wu, wd, gf, *, n1, n2):
    b = x1.shape[0]
    halves = (n2 * SUBLANES) // TOKEN_TILE
    n2h = n2 // halves
    t_shape = (None, FOURIER_WIDTH // LANES, n2, SUBLANES, LANES)
    t_re = pl.BlockSpec(t_shape, lambda bi, i, h: (bi, 0, 0, i, 0))
    t_im = pl.BlockSpec(t_shape, lambda bi, i, h: (bi, 0, 0, n1 // SUBLANES + i, 0))
    blk = lambda w_: pl.BlockSpec((None, n2h, SUBLANES, w_), lambda bi, i, h: (bi, h, i, 0))
    return pl.pallas_call(
        functools.partial(_mixffn_kernel, n2=n2, n2h=n2h),
        grid=(b, n1 // SUBLANES, halves),
        in_specs=[t_re, t_im, blk(D_MODEL), blk(ATTN_WIDTH),
                  pl.BlockSpec((n2h, 2 * n2), lambda bi, i, h: (h, 0)),
                  _const_spec((2 * FOURIER_WIDTH, D_MODEL)), _const_spec((1, D_MODEL)),
                  _const_spec((D_MODEL, D_FF)), _const_spec((D_MODEL, D_FF)), _const_spec((D_FF, D_MODEL)),
                  _const_spec((1, D_MODEL))],
        out_specs=blk(D_MODEL),
        out_shape=jax.ShapeDtypeStruct((b, n2, n1, D_MODEL), F32),
        scratch_shapes=[pltpu.VMEM((FOURIER_WIDTH // LANES, n2h * SUBLANES, LANES), F32)],
        compiler_params=_params(3),
        name="mixffn",
    )(t_arr, t_arr, x1, ya, d2, wo, g2, wg, wu, wd, gf)


def _dft_tables(n1, n2):
    s = n1 * n2
    a1 = 2.0 * np.pi * np.outer(np.arange(n1), np.arange(n1)) / n1
    c1, s1 = np.cos(a1) / np.sqrt(n1), np.sin(a1) / np.sqrt(n1)
    d1 = np.block([[c1, s1], [-s1, c1]])
    a2 = 2.0 * np.pi * np.outer(np.arange(n2), np.arange(n2)) / n2
    d2 = np.concatenate([np.cos(a2), np.sin(a2)], axis=1) / np.sqrt(n2)
    k1 = jnp.arange(n1, dtype=jnp.int32)[None, :]

    def table(mult, count):
        m = (jnp.arange(count, dtype=jnp.int32)[:, None] * mult * k1) % s
        ang = m.astype(F32) * (2.0 * np.pi / s)
        return jnp.broadcast_to(jnp.stack([jnp.cos(ang), jnp.sin(ang)])[..., None], (2, count, n1, LANES))

    tw_a, tw_b = table(SUBLANES, n2 // SUBLANES), table(1, SUBLANES)
    return jnp.asarray(d1, F32).astype(BF16), jnp.asarray(d2, F32).astype(BF16), tw_a, tw_b


def _rope_tables(n1, n2):
    half = HEAD_DIM // 2
    lane = np.arange(LANES)
    inv_freq = ROPE_THETA ** (-(lane % half) / half)
    ang_a = (np.arange(n1) * n2)[:, None] * inv_freq[None, :]
    ang_b = np.arange(n2)[:, None] * inv_freq[None, :]
    sign = np.where((lane % HEAD_DIM) < half, -1.0, 1.0)[None, :]
    rope_a = np.stack([np.cos(ang_a), np.sin(ang_a)])
    rope_b = np.stack([np.cos(ang_b), np.sin(ang_b), sign * np.cos(ang_b), sign * np.sin(ang_b)])
    return jnp.asarray(rope_a, F32), jnp.asarray(rope_b, F32)


LATE_WEIGHTS = ("w_out", "w2_gate", "w2_up", "w2_down")


def _trunk(x, w, *, n1, n2, late_f32=None):
    b, seq, _ = x.shape
    t = b * seq
    d1, d2, tw_a, tw_b = _dft_tables(n1, n2)
    rope_a, rope_b = _rope_tables(n1, n2)
    x1, casts = _ffn(x.reshape(t, D_MODEL), w["g_ffn1"], w["w1_gate"], w["w1_up"], w["w1_down"],
                     cast=tuple(late_f32[n] for n in LATE_WEIGHTS) if late_f32 else ())
    late = dict(zip(LATE_WEIGHTS, casts)) if late_f32 else {n: w[n] for n in LATE_WEIGHTS}
    t_arr, q, k, v = _inproj(x1.reshape(b, n1, n2, D_MODEL), w["g_mix"], w["w_proj"], rope_a, rope_b,
                             tw_a, tw_b, d1, n1=n1, n2=n2)
    ya = _attention(q.reshape(t, ATTN_WIDTH), k.reshape(t, KV_WIDTH), v.reshape(t, KV_WIDTH), w["attn_sink"],
                    seq=seq)
    y = _mixffn(t_arr, x1.reshape(b, n2, n1, D_MODEL), ya.reshape(b, n2, n1, ATTN_WIDTH), d2, late["w_out"],
                w["g_ffn2"], late["w2_gate"], late["w2_up"], late["w2_down"], w["g_final"], n1=n1, n2=n2)
    return y.reshape(b, seq, D_MODEL), late


def kernel(x_prompt, x_sample, g_ffn1, w1_gate, w1_up, w1_down, g_mix, w_in, w_fourier, attn_sink, w_out,
           g_ffn2, w2_gate, w2_up, w2_down, g_final):
    assert g_ffn1.shape[0] == 1, "single-layer trunk"
    w = {
        "g_ffn1": g_ffn1[0][None, :], "g_mix": g_mix[0][None, :], "g_ffn2": g_ffn2[0][None, :],
        "g_final": g_final[None, :],
        "w1_gate": w1_gate[0].astype(BF16), "w1_up": w1_up[0].astype(BF16), "w1_down": w1_down[0].astype(BF16),
        "w_proj": _fold_weights(w_in[0], w_fourier[0]),
        "attn_sink": attn_sink[0],
    }
    late_f32 = {"w_out": w_out[0], "w2_gate": w2_gate[0], "w2_up": w2_up[0], "w2_down": w2_down[0]}
    y_prompt, late = _trunk(x_prompt, w, n1=128, n2=128, late_f32=late_f32)
    w.update(late)
    y_sample, _ = _trunk(x_sample, w, n1=64, n2=64)
    return (y_prompt, y_sample)
```

```python
import functools
import math

import numpy as np
import jax
import jax.numpy as jnp
from jax import lax
from jax.experimental import pallas as pl
from jax.experimental.pallas import tpu as pltpu

D_MODEL = 1024
HEAD_DIM = 64
N_FOURIER_GROUPS = 8
FOURIER_WIDTH = N_FOURIER_GROUPS * HEAD_DIM
N_Q_HEADS = 8
N_KV_HEADS = 2
Q_PER_KV = N_Q_HEADS // N_KV_HEADS
ATTN_WIDTH = N_Q_HEADS * HEAD_DIM
KV_WIDTH = N_KV_HEADS * HEAD_DIM
D_FF = 2816
WINDOW = 128
BLOCK = 128
assert WINDOW == BLOCK, "the band caps in the attention kernel are written for WINDOW == BLOCK"
ROPE_THETA = 10000.0
RMS_EPS = 1e-6
NEG_INF = -1e30

SUBLANES = 8
BF16_SUBLANES = 16
LANES = 128
PROJ_WIDTH = 2 * FOURIER_WIDTH + ATTN_WIDTH + 2 * KV_WIDTH
VMEM_LIMIT_BYTES = 56 * 1024 * 1024
TOKEN_TILE = 512
INPROJ_ROWS = 256

F32 = jnp.float32
BF16 = jnp.bfloat16


def _rms(x, g):
    return x * lax.rsqrt(jnp.mean(x * x, axis=-1, keepdims=True) + RMS_EPS) * g


def _const_spec(shape):
    zeros = (0,) * len(shape)
    return pl.BlockSpec(shape, lambda *_: zeros, pipeline_mode=pl.Buffered(1))


def _params(n_axes):
    return pltpu.CompilerParams(dimension_semantics=("arbitrary",) * n_axes,
                                vmem_limit_bytes=VMEM_LIMIT_BYTES)


FOLD_STEPS = 4


def _fold_kernel(w_in_ref, wf_ref, cc_ref, sc_ref, *rest):
    n_cast = len(rest) // 2
    o_ref = rest[n_cast]
    for src, dst in zip(rest[:n_cast], rest[n_cast + 1:]):
        dst[...] = src[...].astype(BF16)
    hi = lax.Precision.HIGHEST
    step = pl.program_id(0)

    @pl.when(step == 0)
    def _():
        o_ref[:, 2 * FOURIER_WIDTH:] = w_in_ref[:, FOURIER_WIDTH:].astype(BF16)

    for g in range(N_FOURIER_GROUPS):
        @pl.when(step == g // (N_FOURIER_GROUPS // FOLD_STEPS))
        def _(g=g):
            wf = wf_ref[g]
            pr = jnp.dot(cc_ref[...], wf, precision=hi, preferred_element_type=F32)
            pi = jnp.dot(sc_ref[...], wf, precision=hi, preferred_element_type=F32)
            wug = w_in_ref[:, g * HEAD_DIM:(g + 1) * HEAD_DIM]
            o_ref[:, g * HEAD_DIM:(g + 1) * HEAD_DIM] = jnp.dot(
                wug, pr, precision=hi, preferred_element_type=F32).astype(BF16)
            o_ref[:, FOURIER_WIDTH + g * HEAD_DIM:FOURIER_WIDTH + (g + 1) * HEAD_DIM] = (-jnp.dot(
                wug, pi, precision=hi, preferred_element_type=F32)).astype(BF16)


def _fold_weights(w_in, w_fourier, cast=()):
    c = np.arange(HEAD_DIM)
    ang = 2.0 * np.pi * np.outer(c, c) / HEAD_DIM
    scale = HEAD_DIM ** -0.5
    cc = jnp.asarray(np.cos(ang) * scale, F32)
    sc = jnp.asarray(np.sin(ang) * scale, F32)
    cast_specs = [pl.BlockSpec((m.shape[0] // FOLD_STEPS, m.shape[1]), lambda i: (i, 0)) for m in cast]
    out = pl.pallas_call(
        _fold_kernel,
        grid=(FOLD_STEPS,),
        in_specs=[_const_spec(w_in.shape), _const_spec(w_fourier.shape), _const_spec(cc.shape),
                  _const_spec(sc.shape), *cast_specs],
        out_specs=[pl.BlockSpec((D_MODEL, PROJ_WIDTH), lambda i: (0, 0)), *cast_specs],
        out_shape=[jax.ShapeDtypeStruct((D_MODEL, PROJ_WIDTH), BF16),
                   *[jax.ShapeDtypeStruct(m.shape, BF16) for m in cast]],
        compiler_params=_params(1),
        name="fold",
    )(w_in, w_fourier, cc, sc, *cast)
    return out[0], tuple(out[1:])


FF_CHUNKS = ((0, 1024), (1024, 2048), (2048, D_FF))


def _swiglu(xn, wg_ref, wu_ref, wd_ref):
    acc = None
    for lo, hi in FF_CHUNKS:
        gate = jnp.dot(xn, wg_ref[:, lo:hi], preferred_element_type=F32)
        up = jnp.dot(xn, wu_ref[:, lo:hi], preferred_element_type=F32)
        act = (gate * jax.nn.sigmoid(gate) * up).astype(BF16)
        part = jnp.dot(act, wd_ref[lo:hi, :], preferred_element_type=F32)
        acc = part if acc is None else acc + part
    return acc


def _ffn_kernel(x_ref, g_ref, wg_ref, wu_ref, wd_ref, *rest):
    n_cast = len(rest) // 2
    o_ref = rest[n_cast]
    for src, dst in zip(rest[:n_cast], rest[n_cast + 1:]):
        dst[...] = src[...].astype(BF16)
    half = x_ref.shape[0] // 2
    for h in range(2):
        rows = slice(h * half, (h + 1) * half)
        x = x_ref[rows, :]
        xn = _rms(x, g_ref[...]).astype(BF16)
        gate = jnp.dot(xn, wg_ref[...], preferred_element_type=F32)
        up = jnp.dot(xn, wu_ref[...], preferred_element_type=F32)
        act = (gate * jax.nn.sigmoid(gate) * up).astype(BF16)
        o_ref[rows, :] = x + 0.5 * jnp.dot(act, wd_ref[...], preferred_element_type=F32)


def _ffn(x, g, wg, wu, wd, cast=()):
    t = x.shape[0]
    tm = TOKEN_TILE
    steps = t // tm
    cast_specs, cast_shapes = [], []
    for m in cast:
        rows, cols = m.shape
        chunks = math.gcd(rows // BF16_SUBLANES, steps)
        per = steps // chunks
        cast_specs.append(pl.BlockSpec((rows // chunks, cols), lambda i, per=per: (i // per, 0)))
        cast_shapes.append(jax.ShapeDtypeStruct(m.shape, BF16))
    out = pl.pallas_call(
        _ffn_kernel,
        grid=(steps,),
        in_specs=[
            pl.BlockSpec((tm, D_MODEL), lambda i: (i, 0)),
            _const_spec((1, D_MODEL)),
            _const_spec((D_MODEL, D_FF)),
            _const_spec((D_MODEL, D_FF)),
            _const_spec((D_FF, D_MODEL)),
            *cast_specs,
        ],
        out_specs=[pl.BlockSpec((tm, D_MODEL), lambda i: (i, 0)), *cast_specs],
        out_shape=[jax.ShapeDtypeStruct((t, D_MODEL), F32), *cast_shapes],
        compiler_params=_params(1),
        name="ffn",
    )(x, g, wg, wu, wd, *cast)
    return out[0], tuple(out[1:])


def _inproj_kernel(x_ref, g_ref, w_ref, ra_ref, rb_ref, ta_ref, tb_ref, d1_ref,
                   t_ref, q_ref, k_ref, v_ref, z_scr, *, n1):
    tmh = INPROJ_ROWS
    n1h = tmh // SUBLANES
    lane = lax.broadcasted_iota(jnp.int32, (tmh, LANES), 1)
    first_half = (lane % HEAD_DIM) < (HEAD_DIM // 2)
    q0 = 2 * FOURIER_WIDTH
    k0 = q0 + ATTN_WIDTH
    for h in range(n1 // n1h):
        s1s = slice(h * n1h, (h + 1) * n1h)
        x = x_ref[s1s].reshape(tmh, D_MODEL)
        hb = _rms(x, g_ref[...]).astype(BF16)
        proj = jnp.dot(hb, w_ref[...], preferred_element_type=F32)
        for c in range(2 * FOURIER_WIDTH // LANES):
            z_scr[c, h * tmh:(h + 1) * tmh, :] = proj[:, c * LANES:(c + 1) * LANES]

        def rows_a(t):
            return jnp.broadcast_to(t[:, None, :], (n1h, SUBLANES, LANES)).reshape(tmh, LANES)

        def rows_b(t):
            return jnp.broadcast_to(t[None, :, :], (n1h, SUBLANES, LANES)).reshape(tmh, LANES)

        ca, sa = rows_a(ra_ref[0, s1s, :]), rows_a(ra_ref[1, s1s, :])
        cos = ca * rows_b(rb_ref[0]) - sa * rows_b(rb_ref[1])
        sin = sa * rows_b(rb_ref[2]) + ca * rows_b(rb_ref[3])

        def rope(xc):
            rot = jnp.where(first_half, pltpu.roll(xc, LANES - HEAD_DIM // 2, 1), pltpu.roll(xc, HEAD_DIM // 2, 1))
            return xc * cos + rot * sin

        for c in range(ATTN_WIDTH // LANES):
            qc = rope(proj[:, q0 + c * LANES:q0 + (c + 1) * LANES]) * (HEAD_DIM ** -0.5)
            q_ref[s1s, :, c * LANES:(c + 1) * LANES] = qc.reshape(n1h, SUBLANES, LANES)
        k_ref[s1s] = rope(proj[:, k0:k0 + KV_WIDTH]).reshape(n1h, SUBLANES, KV_WIDTH)
        v_ref[s1s] = proj[:, k0 + KV_WIDTH:k0 + 2 * KV_WIDTH].reshape(n1h, SUBLANES, KV_WIDTH)

    d1 = d1_ref[...]
    for j in range(SUBLANES):
        rows = pl.ds(j, n1, stride=SUBLANES)
        nch = FOURIER_WIDTH // LANES
        zr = jnp.concatenate([z_scr[c, rows, :] for c in range(nch)], axis=1)
        zi = jnp.concatenate([z_scr[nch + c, rows, :] for c in range(nch)], axis=1)
        st = jnp.concatenate([zr, zi], axis=0).astype(BF16)
        t = jnp.dot(d1, st, preferred_element_type=F32)
        tr, ti = t[:n1], t[n1:]
        twc = ta_ref[0] * tb_ref[0, j] - ta_ref[1] * tb_ref[1, j]
        tws = ta_ref[1] * tb_ref[0, j] + ta_ref[0] * tb_ref[1, j]
        c4 = jnp.concatenate([twc] * (FOURIER_WIDTH // LANES), axis=1)
        s4 = jnp.concatenate([tws] * (FOURIER_WIDTH // LANES), axis=1)
        t_re, t_im = tr * c4 + ti * s4, ti * c4 - tr * s4
        for c in range(FOURIER_WIDTH // LANES):
            t_ref[c, j, :n1, :] = t_re[:, c * LANES:(c + 1) * LANES]
            t_ref[c, j, n1:, :] = t_im[:, c * LANES:(c + 1) * LANES]


def _inproj(x1, g, w, rope_a, rope_b, tw_a, tw_b, d1, *, n1, n2):
    b = x1.shape[0]
    blk = lambda w_: pl.BlockSpec((None, n1, SUBLANES, w_), lambda bi, i: (bi, 0, i, 0))
    tab_b = pl.BlockSpec((4, SUBLANES, LANES), lambda bi, i: (0, i, 0))
    tw_a_spec = pl.BlockSpec((2, None, n1, LANES), lambda bi, i: (0, i, 0, 0))
    return pl.pallas_call(
        functools.partial(_inproj_kernel, n1=n1),
        grid=(b, n2 // SUBLANES),
        in_specs=[blk(D_MODEL), _const_spec((1, D_MODEL)), _const_spec((D_MODEL, PROJ_WIDTH)),
                  _const_spec((2, n1, LANES)), tab_b, tw_a_spec, _const_spec((2, SUBLANES, n1, LANES)),
                  _const_spec((2 * n1, 2 * n1))],
        out_specs=[
            pl.BlockSpec((None, FOURIER_WIDTH // LANES, SUBLANES, 2 * n1, LANES), lambda bi, i: (bi, 0, i, 0, 0)),
            blk(ATTN_WIDTH), blk(KV_WIDTH), blk(KV_WIDTH),
        ],
        out_shape=[
            jax.ShapeDtypeStruct((b, FOURIER_WIDTH // LANES, n2, 2 * n1, LANES), F32),
            jax.ShapeDtypeStruct((b, n1, n2, ATTN_WIDTH), F32),
            jax.ShapeDtypeStruct((b, n1, n2, KV_WIDTH), F32),
            jax.ShapeDtypeStruct((b, n1, n2, KV_WIDTH), F32),
        ],
        scratch_shapes=[pltpu.VMEM((2 * FOURIER_WIDTH // LANES, n1 * SUBLANES, LANES), F32)],
        compiler_params=_params(2),
        name="inproj",
    )(x1, g, w, rope_a, rope_b, tw_a, tw_b, d1)


F32_MAX = float(np.finfo(np.float32).max)
GROUP_WIDTH = Q_PER_KV * HEAD_DIM


def _attn_kernel(sink_ref, q_ref, kp_ref, kc_ref, kn_ref, vp_ref, vc_ref, vn_ref, o_ref, *, nb, nblk_seq):
    i = pl.program_id(0)
    rows = (nb + 2) * BLOCK
    kfull = jnp.concatenate([kp_ref[...], kc_ref[...], kn_ref[...]], axis=0)
    vfull = jnp.concatenate([vp_ref[...], vc_ref[...], vn_ref[...]], axis=0)
    lo = lax.broadcasted_iota(jnp.int32, (rows, LANES), 1) < HEAD_DIM

    def dup(x):
        r = pltpu.roll(x, HEAD_DIM, 1)
        return jnp.where(lo, x, r), jnp.where(lo, r, x)

    kt = jnp.transpose(kfull).astype(BF16)
    zero_kt = jnp.zeros((HEAD_DIM, 3 * BLOCK), BF16)
    vv = dup(vfull)
    wkeys = 3 * BLOCK
    grp = lax.broadcasted_iota(jnp.int32, (wkeys, GROUP_WIDTH), 1) // HEAD_DIM
    ogrp = lax.broadcasted_iota(jnp.int32, (BLOCK, GROUP_WIDTH), 1) // HEAD_DIM
    kidx = lax.broadcasted_iota(jnp.int32, (BLOCK, BLOCK), 1)
    qidx = lax.broadcasted_iota(jnp.int32, (BLOCK, BLOCK), 0)
    cap_first = jnp.where(kidx >= qidx, F32_MAX, NEG_INF)
    cap_last = jnp.where(kidx <= qidx, F32_MAX, NEG_INF)

    def blockdiag(x):
        wide = jnp.concatenate([x, x], axis=1)
        return jnp.concatenate([jnp.where(grp == g, wide, 0.0) for g in range(Q_PER_KV)], axis=0).astype(BF16)

    def blockdiag_t(xt):
        return jnp.concatenate(
            [jnp.concatenate([xt if gg == g else zero_kt for gg in range(Q_PER_KV)], axis=1)
             for g in range(Q_PER_KV)], axis=0)

    caps = []
    for n in range(nb):
        pb = (i * nb + n) % nblk_seq
        caps.append((jnp.minimum(cap_first, jnp.where(pb > 0, F32_MAX, NEG_INF)),
                     jnp.minimum(cap_last, jnp.where(pb < nblk_seq - 1, F32_MAX, NEG_INF))))

    def scores(n, kh):
        kbd_t = blockdiag_t(kt[kh * HEAD_DIM:(kh + 1) * HEAD_DIM, n * BLOCK:n * BLOCK + wkeys])
        c0 = kh * GROUP_WIDTH
        qh = q_ref[n * BLOCK:(n + 1) * BLOCK, c0:c0 + GROUP_WIDTH].astype(BF16)
        return jnp.dot(qh, kbd_t, preferred_element_type=F32)

    def softmax(n, kh, s_all):
        capf, capl = caps[n]
        tiles, maxes = [], []
        for g in range(Q_PER_KV):
            s0 = jnp.minimum(s_all[:, g * wkeys:g * wkeys + BLOCK], capf)
            s1 = s_all[:, g * wkeys + BLOCK:g * wkeys + 2 * BLOCK]
            s2 = jnp.minimum(s_all[:, g * wkeys + 2 * BLOCK:(g + 1) * wkeys], capl)
            tiles.append((s0, s1, s2))
            maxes.append(jnp.max(jnp.maximum(jnp.maximum(s0, s1), s2), axis=-1, keepdims=True))
        ps, dens = [], []
        for g in range(Q_PER_KV):
            p = [jnp.exp((t - maxes[g]).astype(BF16)) for t in tiles[g]]
            psum = (p[0] + p[1] + p[2]).astype(F32)
            dens.append(jnp.sum(psum, axis=-1, keepdims=True) + jnp.exp(sink_ref[kh * Q_PER_KV + g] - maxes[g]))
            ps += p
        scale = 1.0 / dens[Q_PER_KV - 1]
        for g in range(Q_PER_KV - 2, -1, -1):
            scale = jnp.where(ogrp == g, 1.0 / dens[g], scale)
        return jnp.concatenate(ps, axis=1), scale

    def output(n, kh, p_all, scale):
        vbd = blockdiag(vv[kh][n * BLOCK:n * BLOCK + wkeys])
        c0 = kh * GROUP_WIDTH
        o_ref[n * BLOCK:(n + 1) * BLOCK, c0:c0 + GROUP_WIDTH] = jnp.dot(
            p_all, vbd, preferred_element_type=F32) * scale

    units = [(n, kh) for kh in range(N_KV_HEADS) for n in range(nb)]
    s_prev = None
    for k in range(len(units) + 1):
        s_cur = scores(*units[k]) if k < len(units) else None
        if s_prev is not None:
            output(*units[k - 1], *softmax(*units[k - 1], s_prev))
        s_prev = s_cur


def _attention(q, k, v, sink, *, seq, nb=16):
    t = q.shape[0]
    nblk = t // BLOCK
    tq = nb * BLOCK
    cur = lambda w_: pl.BlockSpec((tq, w_), lambda i: (i, 0))
    prev = pl.BlockSpec((BLOCK, KV_WIDTH), lambda i: (jnp.maximum(i * nb - 1, 0), 0))
    nxt = pl.BlockSpec((BLOCK, KV_WIDTH), lambda i: (jnp.minimum(i * nb + nb, nblk - 1), 0))
    return pl.pallas_call(
        functools.partial(_attn_kernel, nb=nb, nblk_seq=seq // BLOCK),
        grid=(t // tq,),
        in_specs=[pl.BlockSpec(memory_space=pltpu.SMEM), cur(ATTN_WIDTH),
                  prev, cur(KV_WIDTH), nxt, prev, cur(KV_WIDTH), nxt],
        out_specs=cur(ATTN_WIDTH),
        out_shape=jax.ShapeDtypeStruct((t, ATTN_WIDTH), F32),
        compiler_params=_params(1),
        name="attention",
    )(sink, q, k, k, k, v, v, v)


def _mixffn_kernel(tr_ref, ti_ref, x_ref, ya_ref, d2_ref, wo_ref, g2_ref, wg_ref, wu_ref, wd_ref, gf_ref,
                   o_ref, yf_scr, *, n2, n2h):
    tm = n2h * SUBLANES
    nch = FOURIER_WIDTH // LANES

    d2 = d2_ref[...]
    tr_flat = tr_ref.reshape(nch, n2 * SUBLANES, LANES)
    ti_flat = ti_ref.reshape(nch, n2 * SUBLANES, LANES)
    for j in range(SUBLANES):
        rows = pl.ds(j, n2, stride=SUBLANES)
        tr = jnp.concatenate([tr_flat[c, rows, :] for c in range(nch)], axis=1)
        ti = jnp.concatenate([ti_flat[c, rows, :] for c in range(nch)], axis=1)
        st = jnp.concatenate([tr, ti], axis=0).astype(BF16)
        yfj = jnp.dot(d2, st, preferred_element_type=F32)
        for c in range(nch):
            yf_scr[c, pl.ds(j, n2h, stride=SUBLANES), :] = yfj[:, c * LANES:(c + 1) * LANES]
    yf = jnp.concatenate([yf_scr[c] for c in range(nch)], axis=1).astype(BF16)
    ya = ya_ref[...].reshape(tm, ATTN_WIDTH).astype(BF16)
    x2 = (x_ref[...].reshape(tm, D_MODEL)
          + jnp.dot(yf, wo_ref[:FOURIER_WIDTH, :], preferred_element_type=F32)
          + jnp.dot(ya, wo_ref[FOURIER_WIDTH:, :], preferred_element_type=F32))

    half = tm // 2
    for h in range(2):
        xh = x2[h * half:(h + 1) * half]
        xn = _rms(xh, g2_ref[...]).astype(BF16)
        y = _rms(xh + 0.5 * _swiglu(xn, wg_ref, wu_ref, wd_ref), gf_ref[...])
        o_ref[h * (n2h // 2):(h + 1) * (n2h // 2)] = y.reshape(n2h // 2, SUBLANES, D_MODEL)


def _mixffn(t_arr, x1, ya, d2, wo, g2, wg, wu, wd, gf, *, n1, n2):
    b = x1.shape[0]
    halves = (n2 * SUBLANES) // TOKEN_TILE
    n2h = n2 // halves
    t_shape = (None, FOURIER_WIDTH // LANES, n2, SUBLANES, LANES)
    t_re = pl.BlockSpec(t_shape, lambda bi, i, h: (bi, 0, 0, i, 0))
    t_im = pl.BlockSpec(t_shape, lambda bi, i, h: (bi, 0, 0, n1 // SUBLANES + i, 0))
    blk = lambda w_: pl.BlockSpec((None, n2h, SUBLANES, w_), lambda bi, i, h: (bi, h, i, 0))
    return pl.pallas_call(
        functools.partial(_mixffn_kernel, n2=n2, n2h=n2h),
        grid=(b, n1 // SUBLANES, halves),
        in_specs=[t_re, t_im, blk(D_MODEL), blk(ATTN_WIDTH),
                  pl.BlockSpec((n2h, 2 * n2), lambda bi, i, h: (h, 0)),
                  _const_spec((2 * FOURIER_WIDTH, D_MODEL)), _const_spec((1, D_MODEL)),
                  _const_spec((D_MODEL, D_FF)), _const_spec((D_MODEL, D_FF)), _const_spec((D_FF, D_MODEL)),
                  _const_spec((1, D_MODEL))],
        out_specs=blk(D_MODEL),
        out_shape=jax.ShapeDtypeStruct((b, n2, n1, D_MODEL), F32),
        scratch_shapes=[pltpu.VMEM((FOURIER_WIDTH // LANES, n2h * SUBLANES, LANES), F32)],
        compiler_params=_params(3),
        name="mixffn",
    )(t_arr, t_arr, x1, ya, d2, wo, g2, wg, wu, wd, gf)


def _dft_tables(n1, n2):
    s = n1 * n2
    a1 = 2.0 * np.pi * np.outer(np.arange(n1), np.arange(n1)) / n1
    c1, s1 = np.cos(a1) / np.sqrt(n1), np.sin(a1) / np.sqrt(n1)
    d1 = np.block([[c1, s1], [-s1, c1]])
    a2 = 2.0 * np.pi * np.outer(np.arange(n2), np.arange(n2)) / n2
    d2 = np.concatenate([np.cos(a2), np.sin(a2)], axis=1) / np.sqrt(n2)
    k1 = jnp.arange(n1, dtype=jnp.int32)[None, :]

    def table(mult, count):
        m = (jnp.arange(count, dtype=jnp.int32)[:, None] * mult * k1) % s
        ang = m.astype(F32) * (2.0 * np.pi / s)
        return jnp.broadcast_to(jnp.stack([jnp.cos(ang), jnp.sin(ang)])[..., None], (2, count, n1, LANES))

    tw_a, tw_b = table(SUBLANES, n2 // SUBLANES), table(1, SUBLANES)
    return jnp.asarray(d1, F32).astype(BF16), jnp.asarray(d2, F32).astype(BF16), tw_a, tw_b


def _rope_tables(n1, n2):
    half = HEAD_DIM // 2
    lane = np.arange(LANES)
    inv_freq = ROPE_THETA ** (-(lane % half) / half)
    ang_a = (np.arange(n1) * n2)[:, None] * inv_freq[None, :]
    ang_b = np.arange(n2)[:, None] * inv_freq[None, :]
    sign = np.where((lane % HEAD_DIM) < half, -1.0, 1.0)[None, :]
    rope_a = np.stack([np.cos(ang_a), np.sin(ang_a)])
    rope_b = np.stack([np.cos(ang_b), np.sin(ang_b), sign * np.cos(ang_b), sign * np.sin(ang_b)])
    return jnp.asarray(rope_a, F32), jnp.asarray(rope_b, F32)


LATE_WEIGHTS = ("w_out", "w2_gate", "w2_up", "w2_down")


def _trunk(x, w, *, n1, n2, late_f32=None):
    b, seq, _ = x.shape
    t = b * seq
    d1, d2, tw_a, tw_b = _dft_tables(n1, n2)
    rope_a, rope_b = _rope_tables(n1, n2)
    x1, casts = _ffn(x.reshape(t, D_MODEL), w["g_ffn1"], w["w1_gate"], w["w1_up"], w["w1_down"],
                     cast=tuple(late_f32[n] for n in LATE_WEIGHTS) if late_f32 else ())
    late = dict(zip(LATE_WEIGHTS, casts)) if late_f32 else {n: w[n] for n in LATE_WEIGHTS}
    t_arr, q, k, v = _inproj(x1.reshape(b, n1, n2, D_MODEL), w["g_mix"], w["w_proj"], rope_a, rope_b,
                             tw_a, tw_b, d1, n1=n1, n2=n2)
    ya = _attention(q.reshape(t, ATTN_WIDTH), k.reshape(t, KV_WIDTH), v.reshape(t, KV_WIDTH), w["attn_sink"],
                    seq=seq)
    y = _mixffn(t_arr, x1.reshape(b, n2, n1, D_MODEL), ya.reshape(b, n2, n1, ATTN_WIDTH), d2, late["w_out"],
                w["g_ffn2"], late["w2_gate"], late["w2_up"], late["w2_down"], w["g_final"], n1=n1, n2=n2)
    return y.reshape(b, seq, D_MODEL), late


def kernel(x_prompt, x_sample, g_ffn1, w1_gate, w1_up, w1_down, g_mix, w_in, w_fourier, attn_sink, w_out,
           g_ffn2, w2_gate, w2_up, w2_down, g_final):
    assert g_ffn1.shape[0] == 1, "single-layer trunk"
    w_proj, w1 = _fold_weights(w_in[0], w_fourier[0], cast=(w1_gate[0], w1_up[0], w1_down[0]))
    w = {
        "g_ffn1": g_ffn1[0][None, :], "g_mix": g_mix[0][None, :], "g_ffn2": g_ffn2[0][None, :],
        "g_final": g_final[None, :],
        "w_proj": w_proj, "w1_gate": w1[0], "w1_up": w1[1], "w1_down": w1[2],
        "attn_sink": attn_sink[0],
    }
    late_f32 = {"w_out": w_out[0], "w2_gate": w2_gate[0], "w2_up": w2_up[0], "w2_down": w2_down[0]}
    y_prompt, late = _trunk(x_prompt, w, n1=128, n2=128, late_f32=late_f32)
    w.update(late)
    y_sample, _ = _trunk(x_sample, w, n1=64, n2=64)
    return (y_prompt, y_sample)
```

```python
import functools
import math

import numpy as np
import jax
import jax.numpy as jnp
from jax import lax
from jax.experimental import pallas as pl
from jax.experimental.pallas import tpu as pltpu

D_MODEL = 1024
HEAD_DIM = 64
N_FOURIER_GROUPS = 8
FOURIER_WIDTH = N_FOURIER_GROUPS * HEAD_DIM
N_Q_HEADS = 8
N_KV_HEADS = 2
Q_PER_KV = N_Q_HEADS // N_KV_HEADS
ATTN_WIDTH = N_Q_HEADS * HEAD_DIM
KV_WIDTH = N_KV_HEADS * HEAD_DIM
D_FF = 2816
WINDOW = 128
BLOCK = 128
assert WINDOW == BLOCK, "the band caps in the attention kernel are written for WINDOW == BLOCK"
ROPE_THETA = 10000.0
RMS_EPS = 1e-6
NEG_INF = -1e30

SUBLANES = 8
BF16_SUBLANES = 16
LANES = 128
PROJ_WIDTH = 2 * FOURIER_WIDTH + ATTN_WIDTH + 2 * KV_WIDTH
VMEM_LIMIT_BYTES = 56 * 1024 * 1024
TOKEN_TILE = 512
INPROJ_ROWS = 256

F32 = jnp.float32
BF16 = jnp.bfloat16


def _rms(x, g):
    return x * lax.rsqrt(jnp.mean(x * x, axis=-1, keepdims=True) + RMS_EPS) * g


def _const_spec(shape):
    zeros = (0,) * len(shape)
    return pl.BlockSpec(shape, lambda *_: zeros, pipeline_mode=pl.Buffered(1))


def _params(n_axes):
    return pltpu.CompilerParams(dimension_semantics=("arbitrary",) * n_axes,
                                vmem_limit_bytes=VMEM_LIMIT_BYTES)


FOLD_STEPS = 8


def _fold_kernel(w_in_ref, wf_ref, cc_ref, sc_ref, *rest):
    n_cast = len(rest) // 2
    o_ref = rest[n_cast]
    for src, dst in zip(rest[:n_cast], rest[n_cast + 1:]):
        dst[...] = src[...].astype(BF16)
    hi = lax.Precision.HIGHEST
    step = pl.program_id(0)

    @pl.when(step == 0)
    def _():
        o_ref[:, 2 * FOURIER_WIDTH:] = w_in_ref[:, FOURIER_WIDTH:].astype(BF16)

    for g in range(N_FOURIER_GROUPS):
        @pl.when(step == g // (N_FOURIER_GROUPS // FOLD_STEPS))
        def _(g=g):
            wf = wf_ref[g]
            pr = jnp.dot(cc_ref[...], wf, precision=hi, preferred_element_type=F32)
            pi = jnp.dot(sc_ref[...], wf, precision=hi, preferred_element_type=F32)
            wug = w_in_ref[:, g * HEAD_DIM:(g + 1) * HEAD_DIM]
            o_ref[:, g * HEAD_DIM:(g + 1) * HEAD_DIM] = jnp.dot(
                wug, pr, precision=hi, preferred_element_type=F32).astype(BF16)
            o_ref[:, FOURIER_WIDTH + g * HEAD_DIM:FOURIER_WIDTH + (g + 1) * HEAD_DIM] = (-jnp.dot(
                wug, pi, precision=hi, preferred_element_type=F32)).astype(BF16)


def _fold_weights(w_in, w_fourier, cast=()):
    c = np.arange(HEAD_DIM)
    ang = 2.0 * np.pi * np.outer(c, c) / HEAD_DIM
    scale = HEAD_DIM ** -0.5
    cc = jnp.asarray(np.cos(ang) * scale, F32)
    sc = jnp.asarray(np.sin(ang) * scale, F32)
    cast_specs = [pl.BlockSpec((m.shape[0] // FOLD_STEPS, m.shape[1]), lambda i: (i, 0)) for m in cast]
    out = pl.pallas_call(
        _fold_kernel,
        grid=(FOLD_STEPS,),
        in_specs=[_const_spec(w_in.shape), _const_spec(w_fourier.shape), _const_spec(cc.shape),
                  _const_spec(sc.shape), *cast_specs],
        out_specs=[pl.BlockSpec((D_MODEL, PROJ_WIDTH), lambda i: (0, 0)), *cast_specs],
        out_shape=[jax.ShapeDtypeStruct((D_MODEL, PROJ_WIDTH), BF16),
                   *[jax.ShapeDtypeStruct(m.shape, BF16) for m in cast]],
        compiler_params=_params(1),
        name="fold",
    )(w_in, w_fourier, cc, sc, *cast)
    return out[0], tuple(out[1:])


FF_CHUNKS = ((0, 1024), (1024, 2048), (2048, D_FF))


def _swiglu(xn, wg_ref, wu_ref, wd_ref):
    acc = None
    for lo, hi in FF_CHUNKS:
        gate = jnp.dot(xn, wg_ref[:, lo:hi], preferred_element_type=F32)
        up = jnp.dot(xn, wu_ref[:, lo:hi], preferred_element_type=F32)
        act = (gate * jax.nn.sigmoid(gate) * up).astype(BF16)
        part = jnp.dot(act, wd_ref[lo:hi, :], preferred_element_type=F32)
        acc = part if acc is None else acc + part
    return acc


def _ffn_kernel(x_ref, g_ref, wg_ref, wu_ref, wd_ref, *rest):
    n_cast = len(rest) // 2
    o_ref = rest[n_cast]
    for src, dst in zip(rest[:n_cast], rest[n_cast + 1:]):
        dst[...] = src[...].astype(BF16)
    half = x_ref.shape[0] // 2
    for h in range(2):
        rows = slice(h * half, (h + 1) * half)
        x = x_ref[rows, :]
        xn = _rms(x, g_ref[...]).astype(BF16)
        gate = jnp.dot(xn, wg_ref[...], preferred_element_type=F32)
        up = jnp.dot(xn, wu_ref[...], preferred_element_type=F32)
        act = (gate * jax.nn.sigmoid(gate) * up).astype(BF16)
        o_ref[rows, :] = x + 0.5 * jnp.dot(act, wd_ref[...], preferred_element_type=F32)


def _ffn(x, g, wg, wu, wd, cast=()):
    t = x.shape[0]
    tm = TOKEN_TILE
    steps = t // tm
    cast_specs, cast_shapes = [], []
    for m in cast:
        rows, cols = m.shape
        chunks = math.gcd(rows // BF16_SUBLANES, steps)
        per = steps // chunks
        cast_specs.append(pl.BlockSpec((rows // chunks, cols), lambda i, per=per: (i // per, 0)))
        cast_shapes.append(jax.ShapeDtypeStruct(m.shape, BF16))
    out = pl.pallas_call(
        _ffn_kernel,
        grid=(steps,),
        in_specs=[
            pl.BlockSpec((tm, D_MODEL), lambda i: (i, 0)),
            _const_spec((1, D_MODEL)),
            _const_spec((D_MODEL, D_FF)),
            _const_spec((D_MODEL, D_FF)),
            _const_spec((D_FF, D_MODEL)),
            *cast_specs,
        ],
        out_specs=[pl.BlockSpec((tm, D_MODEL), lambda i: (i, 0)), *cast_specs],
        out_shape=[jax.ShapeDtypeStruct((t, D_MODEL), F32), *cast_shapes],
        compiler_params=_params(1),
        name="ffn",
    )(x, g, wg, wu, wd, *cast)
    return out[0], tuple(out[1:])


def _inproj_kernel(x_ref, g_ref, w_ref, ra_ref, rb_ref, ta_ref, tb_ref, d1_ref,
                   t_ref, q_ref, k_ref, v_ref, z_scr, *, n1):
    tmh = INPROJ_ROWS
    n1h = tmh // SUBLANES
    lane = lax.broadcasted_iota(jnp.int32, (tmh, LANES), 1)
    first_half = (lane % HEAD_DIM) < (HEAD_DIM // 2)
    q0 = 2 * FOURIER_WIDTH
    k0 = q0 + ATTN_WIDTH
    for h in range(n1 // n1h):
        s1s = slice(h * n1h, (h + 1) * n1h)
        x = x_ref[s1s].reshape(tmh, D_MODEL)
        hb = _rms(x, g_ref[...]).astype(BF16)
        proj = jnp.dot(hb, w_ref[...], preferred_element_type=F32)
        for c in range(2 * FOURIER_WIDTH // LANES):
            z_scr[c, h * tmh:(h + 1) * tmh, :] = proj[:, c * LANES:(c + 1) * LANES]

        def rows_a(t):
            return jnp.broadcast_to(t[:, None, :], (n1h, SUBLANES, LANES)).reshape(tmh, LANES)

        def rows_b(t):
            return jnp.broadcast_to(t[None, :, :], (n1h, SUBLANES, LANES)).reshape(tmh, LANES)

        ca, sa = rows_a(ra_ref[0, s1s, :]), rows_a(ra_ref[1, s1s, :])
        cos = ca * rows_b(rb_ref[0]) - sa * rows_b(rb_ref[1])
        sin = sa * rows_b(rb_ref[2]) + ca * rows_b(rb_ref[3])

        def rope(xc):
            rot = jnp.where(first_half, pltpu.roll(xc, LANES - HEAD_DIM // 2, 1), pltpu.roll(xc, HEAD_DIM // 2, 1))
            return xc * cos + rot * sin

        for c in range(ATTN_WIDTH // LANES):
            qc = rope(proj[:, q0 + c * LANES:q0 + (c + 1) * LANES]) * (HEAD_DIM ** -0.5)
            q_ref[s1s, :, c * LANES:(c + 1) * LANES] = qc.reshape(n1h, SUBLANES, LANES)
        k_ref[s1s] = rope(proj[:, k0:k0 + KV_WIDTH]).reshape(n1h, SUBLANES, KV_WIDTH)
        v_ref[s1s] = proj[:, k0 + KV_WIDTH:k0 + 2 * KV_WIDTH].reshape(n1h, SUBLANES, KV_WIDTH)

    d1 = d1_ref[...]
    for j in range(SUBLANES):
        rows = pl.ds(j, n1, stride=SUBLANES)
        nch = FOURIER_WIDTH // LANES
        zr = jnp.concatenate([z_scr[c, rows, :] for c in range(nch)], axis=1)
        zi = jnp.concatenate([z_scr[nch + c, rows, :] for c in range(nch)], axis=1)
        st = jnp.concatenate([zr, zi], axis=0).astype(BF16)
        t = jnp.dot(d1, st, preferred_element_type=F32)
        tr, ti = t[:n1], t[n1:]
        twc = ta_ref[0] * tb_ref[0, j] - ta_ref[1] * tb_ref[1, j]
        tws = ta_ref[1] * tb_ref[0, j] + ta_ref[0] * tb_ref[1, j]
        c4 = jnp.concatenate([twc] * (FOURIER_WIDTH // LANES), axis=1)
        s4 = jnp.concatenate([tws] * (FOURIER_WIDTH // LANES), axis=1)
        t_re, t_im = tr * c4 + ti * s4, ti * c4 - tr * s4
        for c in range(FOURIER_WIDTH // LANES):
            t_ref[c, j, :n1, :] = t_re[:, c * LANES:(c + 1) * LANES]
            t_ref[c, j, n1:, :] = t_im[:, c * LANES:(c + 1) * LANES]


def _inproj(x1, g, w, rope_a, rope_b, tw_a, tw_b, d1, *, n1, n2):
    b = x1.shape[0]
    blk = lambda w_: pl.BlockSpec((None, n1, SUBLANES, w_), lambda bi, i: (bi, 0, i, 0))
    tab_b = pl.BlockSpec((4, SUBLANES, LANES), lambda bi, i: (0, i, 0))
    tw_a_spec = pl.BlockSpec((2, None, n1, LANES), lambda bi, i: (0, i, 0, 0))
    return pl.pallas_call(
        functools.partial(_inproj_kernel, n1=n1),
        grid=(b, n2 // SUBLANES),
        in_specs=[blk(D_MODEL), _const_spec((1, D_MODEL)), _const_spec((D_MODEL, PROJ_WIDTH)),
                  _const_spec((2, n1, LANES)), tab_b, tw_a_spec, _const_spec((2, SUBLANES, n1, LANES)),
                  _const_spec((2 * n1, 2 * n1))],
        out_specs=[
            pl.BlockSpec((None, FOURIER_WIDTH // LANES, SUBLANES, 2 * n1, LANES), lambda bi, i: (bi, 0, i, 0, 0)),
            blk(ATTN_WIDTH), blk(KV_WIDTH), blk(KV_WIDTH),
        ],
        out_shape=[
            jax.ShapeDtypeStruct((b, FOURIER_WIDTH // LANES, n2, 2 * n1, LANES), F32),
            jax.ShapeDtypeStruct((b, n1, n2, ATTN_WIDTH), F32),
            jax.ShapeDtypeStruct((b, n1, n2, KV_WIDTH), F32),
            jax.ShapeDtypeStruct((b, n1, n2, KV_WIDTH), F32),
        ],
        scratch_shapes=[pltpu.VMEM((2 * FOURIER_WIDTH // LANES, n1 * SUBLANES, LANES), F32)],
        compiler_params=_params(2),
        name="inproj",
    )(x1, g, w, rope_a, rope_b, tw_a, tw_b, d1)


F32_MAX = float(np.finfo(np.float32).max)
GROUP_WIDTH = Q_PER_KV * HEAD_DIM


def _attn_kernel(sink_ref, q_ref, kp_ref, kc_ref, kn_ref, vp_ref, vc_ref, vn_ref, o_ref, *, nb, nblk_seq):
    i = pl.program_id(0)
    rows = (nb + 2) * BLOCK
    kfull = jnp.concatenate([kp_ref[...], kc_ref[...], kn_ref[...]], axis=0)
    vfull = jnp.concatenate([vp_ref[...], vc_ref[...], vn_ref[...]], axis=0)
    lo = lax.broadcasted_iota(jnp.int32, (rows, LANES), 1) < HEAD_DIM

    def dup(x):
        r = pltpu.roll(x, HEAD_DIM, 1)
        return jnp.where(lo, x, r), jnp.where(lo, r, x)

    kt = jnp.transpose(kfull).astype(BF16)
    zero_kt = jnp.zeros((HEAD_DIM, 3 * BLOCK), BF16)
    vv = dup(vfull)
    wkeys = 3 * BLOCK
    grp = lax.broadcasted_iota(jnp.int32, (wkeys, GROUP_WIDTH), 1) // HEAD_DIM
    ogrp = lax.broadcasted_iota(jnp.int32, (BLOCK, GROUP_WIDTH), 1) // HEAD_DIM
    kidx = lax.broadcasted_iota(jnp.int32, (BLOCK, BLOCK), 1)
    qidx = lax.broadcasted_iota(jnp.int32, (BLOCK, BLOCK), 0)
    cap_first = jnp.where(kidx >= qidx, F32_MAX, NEG_INF)
    cap_last = jnp.where(kidx <= qidx, F32_MAX, NEG_INF)

    def blockdiag(x):
        wide = jnp.concatenate([x, x], axis=1)
        return jnp.concatenate([jnp.where(grp == g, wide, 0.0) for g in range(Q_PER_KV)], axis=0).astype(BF16)

    def blockdiag_t(xt):
        return jnp.concatenate(
            [jnp.concatenate([xt if gg == g else zero_kt for gg in range(Q_PER_KV)], axis=1)
             for g in range(Q_PER_KV)], axis=0)

    caps = []
    for n in range(nb):
        pb = (i * nb + n) % nblk_seq
        caps.append((jnp.minimum(cap_first, jnp.where(pb > 0, F32_MAX, NEG_INF)),
                     jnp.minimum(cap_last, jnp.where(pb < nblk_seq - 1, F32_MAX, NEG_INF))))

    def scores(n, kh):
        kbd_t = blockdiag_t(kt[kh * HEAD_DIM:(kh + 1) * HEAD_DIM, n * BLOCK:n * BLOCK + wkeys])
        c0 = kh * GROUP_WIDTH
        qh = q_ref[n * BLOCK:(n + 1) * BLOCK, c0:c0 + GROUP_WIDTH].astype(BF16)
        return jnp.dot(qh, kbd_t, preferred_element_type=F32)

    def softmax(n, kh, s_all):
        capf, capl = caps[n]
        tiles, maxes = [], []
        for g in range(Q_PER_KV):
            s0 = jnp.minimum(s_all[:, g * wkeys:g * wkeys + BLOCK], capf)
            s1 = s_all[:, g * wkeys + BLOCK:g * wkeys + 2 * BLOCK]
            s2 = jnp.minimum(s_all[:, g * wkeys + 2 * BLOCK:(g + 1) * wkeys], capl)
            tiles.append((s0, s1, s2))
            maxes.append(jnp.max(jnp.maximum(jnp.maximum(s0, s1), s2), axis=-1, keepdims=True))
        ps, dens = [], []
        for g in range(Q_PER_KV):
            p = [jnp.exp((t - maxes[g]).astype(BF16)) for t in tiles[g]]
            psum = (p[0] + p[1] + p[2]).astype(F32)
            dens.append(jnp.sum(psum, axis=-1, keepdims=True) + jnp.exp(sink_ref[kh * Q_PER_KV + g] - maxes[g]))
            ps += p
        scale = 1.0 / dens[Q_PER_KV - 1]
        for g in range(Q_PER_KV - 2, -1, -1):
            scale = jnp.where(ogrp == g, 1.0 / dens[g], scale)
        return jnp.concatenate(ps, axis=1), scale

    def output(n, kh, p_all, scale):
        vbd = blockdiag(vv[kh][n * BLOCK:n * BLOCK + wkeys])
        c0 = kh * GROUP_WIDTH
        o_ref[n * BLOCK:(n + 1) * BLOCK, c0:c0 + GROUP_WIDTH] = jnp.dot(
            p_all, vbd, preferred_element_type=F32) * scale

    units = [(n, kh) for kh in range(N_KV_HEADS) for n in range(nb)]
    s_prev = None
    for k in range(len(units) + 1):
        s_cur = scores(*units[k]) if k < len(units) else None
        if s_prev is not None:
            output(*units[k - 1], *softmax(*units[k - 1], s_prev))
        s_prev = s_cur


def _attention(q, k, v, sink, *, seq, nb=16):
    t = q.shape[0]
    nblk = t // BLOCK
    tq = nb * BLOCK
    cur = lambda w_: pl.BlockSpec((tq, w_), lambda i: (i, 0))
    prev = pl.BlockSpec((BLOCK, KV_WIDTH), lambda i: (jnp.maximum(i * nb - 1, 0), 0))
    nxt = pl.BlockSpec((BLOCK, KV_WIDTH), lambda i: (jnp.minimum(i * nb + nb, nblk - 1), 0))
    return pl.pallas_call(
        functools.partial(_attn_kernel, nb=nb, nblk_seq=seq // BLOCK),
        grid=(t // tq,),
        in_specs=[pl.BlockSpec(memory_space=pltpu.SMEM), cur(ATTN_WIDTH),
                  prev, cur(KV_WIDTH), nxt, prev, cur(KV_WIDTH), nxt],
        out_specs=cur(ATTN_WIDTH),
        out_shape=jax.ShapeDtypeStruct((t, ATTN_WIDTH), F32),
        compiler_params=_params(1),
        name="attention",
    )(sink, q, k, k, k, v, v, v)


def _mixffn_kernel(tr_ref, ti_ref, x_ref, ya_ref, d2_ref, wo_ref, g2_ref, wg_ref, wu_ref, wd_ref, gf_ref,
                   o_ref, yf_scr, *, n2, n2h):
    tm = n2h * SUBLANES
    nch = FOURIER_WIDTH // LANES

    d2 = d2_ref[...]
    tr_flat = tr_ref.reshape(nch, n2 * SUBLANES, LANES)
    ti_flat = ti_ref.reshape(nch, n2 * SUBLANES, LANES)
    for j in range(SUBLANES):
        rows = pl.ds(j, n2, stride=SUBLANES)
        tr = jnp.concatenate([tr_flat[c, rows, :] for c in range(nch)], axis=1)
        ti = jnp.concatenate([ti_flat[c, rows, :] for c in range(nch)], axis=1)
        st = jnp.concatenate([tr, ti], axis=0).astype(BF16)
        yfj = jnp.dot(d2, st, preferred_element_type=F32)
        for c in range(nch):
            yf_scr[c, pl.ds(j, n2h, stride=SUBLANES), :] = yfj[:, c * LANES:(c + 1) * LANES]
    yf = jnp.concatenate([yf_scr[c] for c in range(nch)], axis=1).astype(BF16)
    ya = ya_ref[...].reshape(tm, ATTN_WIDTH).astype(BF16)
    x2 = (x_ref[...].reshape(tm, D_MODEL)
          + jnp.dot(yf, wo_ref[:FOURIER_WIDTH, :], preferred_element_type=F32)
          + jnp.dot(ya, wo_ref[FOURIER_WIDTH:, :], preferred_element_type=F32))

    half = tm // 2
    for h in range(2):
        xh = x2[h * half:(h + 1) * half]
        xn = _rms(xh, g2_ref[...]).astype(BF16)
        y = _rms(xh + 0.5 * _swiglu(xn, wg_ref, wu_ref, wd_ref), gf_ref[...])
        o_ref[h * (n2h // 2):(h + 1) * (n2h // 2)] = y.reshape(n2h // 2, SUBLANES, D_MODEL)


def _mixffn(t_arr, x1, ya, d2, wo, g2, wg, wu, wd, gf, *, n1, n2):
    b = x1.shape[0]
    halves = (n2 * SUBLANES) // TOKEN_TILE
    n2h = n2 // halves
    t_shape = (None, FOURIER_WIDTH // LANES, n2, SUBLANES, LANES)
    t_re = pl.BlockSpec(t_shape, lambda bi, i, h: (bi, 0, 0, i, 0))
    t_im = pl.BlockSpec(t_shape, lambda bi, i, h: (bi, 0, 0, n1 // SUBLANES + i, 0))
    blk = lambda w_: pl.BlockSpec((None, n2h, SUBLANES, w_), lambda bi, i, h: (bi, h, i, 0))
    return pl.pallas_call(
        functools.partial(_mixffn_kernel, n2=n2, n2h=n2h),
        grid=(b, n1 // SUBLANES, halves),
        in_specs=[t_re, t_im, blk(D_MODEL), blk(ATTN_WIDTH),
                  pl.BlockSpec((n2h, 2 * n2), lambda bi, i, h: (h, 0)),
                  _const_spec((2 * FOURIER_WIDTH, D_MODEL)), _const_spec((1, D_MODEL)),
                  _const_spec((D_MODEL, D_FF)), _const_spec((D_MODEL, D_FF)), _const_spec((D_FF, D_MODEL)),
                  _const_spec((1, D_MODEL))],
        out_specs=blk(D_MODEL),
        out_shape=jax.ShapeDtypeStruct((b, n2, n1, D_MODEL), F32),
        scratch_shapes=[pltpu.VMEM((FOURIER_WIDTH // LANES, n2h * SUBLANES, LANES), F32)],
        compiler_params=_params(3),
        name="mixffn",
    )(t_arr, t_arr, x1, ya, d2, wo, g2, wg, wu, wd, gf)


def _dft_tables(n1, n2):
    s = n1 * n2
    a1 = 2.0 * np.pi * np.outer(np.arange(n1), np.arange(n1)) / n1
    c1, s1 = np.cos(a1) / np.sqrt(n1), np.sin(a1) / np.sqrt(n1)
    d1 = np.block([[c1, s1], [-s1, c1]])
    a2 = 2.0 * np.pi * np.outer(np.arange(n2), np.arange(n2)) / n2
    d2 = np.concatenate([np.cos(a2), np.sin(a2)], axis=1) / np.sqrt(n2)
    k1 = jnp.arange(n1, dtype=jnp.int32)[None, :]

    def table(mult, count):
        m = (jnp.arange(count, dtype=jnp.int32)[:, None] * mult * k1) % s
        ang = m.astype(F32) * (2.0 * np.pi / s)
        return jnp.broadcast_to(jnp.stack([jnp.cos(ang), jnp.sin(ang)])[..., None], (2, count, n1, LANES))

    tw_a, tw_b = table(SUBLANES, n2 // SUBLANES), table(1, SUBLANES)
    return jnp.asarray(d1, F32).astype(BF16), jnp.asarray(d2, F32).astype(BF16), tw_a, tw_b


def _rope_tables(n1, n2):
    half = HEAD_DIM // 2
    lane = np.arange(LANES)
    inv_freq = ROPE_THETA ** (-(lane % half) / half)
    ang_a = (np.arange(n1) * n2)[:, None] * inv_freq[None, :]
    ang_b = np.arange(n2)[:, None] * inv_freq[None, :]
    sign = np.where((lane % HEAD_DIM) < half, -1.0, 1.0)[None, :]
    rope_a = np.stack([np.cos(ang_a), np.sin(ang_a)])
    rope_b = np.stack([np.cos(ang_b), np.sin(ang_b), sign * np.cos(ang_b), sign * np.sin(ang_b)])
    return jnp.asarray(rope_a, F32), jnp.asarray(rope_b, F32)


LATE_WEIGHTS = ("w_out", "w2_gate", "w2_up", "w2_down")


def _trunk(x, w, *, n1, n2, late_f32=None):
    b, seq, _ = x.shape
    t = b * seq
    d1, d2, tw_a, tw_b = _dft_tables(n1, n2)
    rope_a, rope_b = _rope_tables(n1, n2)
    x1, casts = _ffn(x.reshape(t, D_MODEL), w["g_ffn1"], w["w1_gate"], w["w1_up"], w["w1_down"],
                     cast=tuple(late_f32[n] for n in LATE_WEIGHTS) if late_f32 else ())
    late = dict(zip(LATE_WEIGHTS, casts)) if late_f32 else {n: w[n] for n in LATE_WEIGHTS}
    t_arr, q, k, v = _inproj(x1.reshape(b, n1, n2, D_MODEL), w["g_mix"], w["w_proj"], rope_a, rope_b,
                             tw_a, tw_b, d1, n1=n1, n2=n2)
    ya = _attention(q.reshape(t, ATTN_WIDTH), k.reshape(t, KV_WIDTH), v.reshape(t, KV_WIDTH), w["attn_sink"],
                    seq=seq)
    y = _mixffn(t_arr, x1.reshape(b, n2, n1, D_MODEL), ya.reshape(b, n2, n1, ATTN_WIDTH), d2, late["w_out"],
                w["g_ffn2"], late["w2_gate"], late["w2_up"], late["w2_down"], w["g_final"], n1=n1, n2=n2)
    return y.reshape(b, seq, D_MODEL), late


def kernel(x_prompt, x_sample, g_ffn1, w1_gate, w1_up, w1_down, g_mix, w_in, w_fourier, attn_sink, w_out,
           g_ffn2, w2_gate, w2_up, w2_down, g_final):
    assert g_ffn1.shape[0] == 1, "single-layer trunk"
    w_proj, w1 = _fold_weights(w_in[0], w_fourier[0], cast=(w1_gate[0], w1_up[0], w1_down[0]))
    w = {
        "g_ffn1": g_ffn1[0][None, :], "g_mix": g_mix[0][None, :], "g_ffn2": g_ffn2[0][None, :],
        "g_final": g_final[None, :],
        "w_proj": w_proj, "w1_gate": w1[0], "w1_up": w1[1], "w1_down": w1[2],
        "attn_sink": attn_sink[0],
    }
    late_f32 = {"w_out": w_out[0], "w2_gate": w2_gate[0], "w2_up": w2_up[0], "w2_down": w2_down[0]}
    y_prompt, late = _trunk(x_prompt, w, n1=128, n2=128, late_f32=late_f32)
    w.update(late)
    y_sample, _ = _trunk(x_sample, w, n1=64, n2=64)
    return (y_prompt, y_sample)
```
